```python
import math
import jax
import jax.numpy as jnp
from jax import lax
import numpy as np

D_MODEL = 1024
BATCH = 32
SEQ = 2048
DEPTH = 2
DEC_BATCH = 128
DEC_SEQ = 8
PAST_LEN = 16384
PAGE_SIZE = 128

NSA_HEADS = 8
NSA_GROUPS = 2
NSA_HEAD_DIM = 64
NSA_REP = NSA_HEADS // NSA_GROUPS
NSA_BLOCK = 64
NSA_TOP_K = 16
NSA_WINDOW = 512
MLA_HEADS = 8
MLA_Q_RANK = 256
MLA_KV_RANK = 128
MLA_NOPE_DIM = 64
MLA_ROPE_DIM = 32
MLA_V_DIM = 64
MLA_LAT_W = MLA_KV_RANK + MLA_ROPE_DIM
D_FF = 2816
ROPE_THETA = 10000.0
RMS_EPS = 1e-6
Q_BLOCK = 128
NSA_KV_W = NSA_GROUPS * NSA_HEAD_DIM
IN_SPLIT_SIZES = (NSA_HEADS * NSA_HEAD_DIM, 6 * NSA_KV_W, 3 * NSA_HEADS, MLA_Q_RANK, MLA_KV_RANK, MLA_ROPE_DIM, 2 * D_MODEL)
IN_WIDTH = sum(IN_SPLIT_SIZES)

kernel_name = 'nsa_mla_macaron_hybrid_step'


def rms_norm(x, g):
    xf = x.astype(jnp.float32)
    y = xf * lax.rsqrt(jnp.mean(xf * xf, axis=-1, keepdims=True) + RMS_EPS)
    return (y * g.astype(jnp.float32)).astype(x.dtype)


def rope(x, pos):
    d = x.shape[-1]
    inv = ROPE_THETA ** (-jnp.arange(0, d, 2, dtype=jnp.float32) / d)
    ang = pos.astype(jnp.float32)[:, None] * inv[None, :]
    cos = jnp.cos(ang)[None, :, None, :]
    sin = jnp.sin(ang)[None, :, None, :]
    xf = x.astype(jnp.float32)
    x1, x2 = xf[..., : d // 2], xf[..., d // 2:]
    return jnp.concatenate([x1 * cos - x2 * sin, x2 * cos + x1 * sin], axis=-1).astype(x.dtype)


def swiglu(x, w13, w2):
    gate, up = jnp.split(x @ w13, 2, axis=-1)
    return (jax.nn.silu(gate) * up) @ w2


def masked_softmax(s, mask):
    s = jnp.where(mask, s, -jnp.inf)
    m = jnp.max(s, axis=-1, keepdims=True)
    m = jnp.where(jnp.isfinite(m), m, 0.0)
    e = jnp.where(mask, jnp.exp(s - m), 0.0)
    return e / jnp.maximum(jnp.sum(e, axis=-1, keepdims=True), 1e-30)


def pad_seq(k, n):
    return jnp.pad(k, [(0, 0), (0, n - k.shape[1])] + [(0, 0)] * (k.ndim - 2))


def block_means(k, nb):
    B = k.shape[0]
    kb = pad_seq(k, nb * NSA_BLOCK).reshape(B, nb, NSA_BLOCK, *k.shape[2:])
    return jnp.mean(kb.astype(jnp.float32), axis=2).astype(k.dtype)


def to_blocks(k, nb):
    B = k.shape[0]
    kb = pad_seq(k, nb * NSA_BLOCK).reshape(B, nb, NSA_BLOCK, *k.shape[2:])
    return jnp.transpose(kb, (0, 3, 1, 2, 4))


def project(h, pos, w_in, g_cq, w_uq, g_ckv, w_uk):
    B, T, _ = h.shape
    G, D = NSA_GROUPS, NSA_HEAD_DIM
    pts = np.cumsum(IN_SPLIT_SIZES)[:-1].tolist()
    q_a, kv_a, gate_a, c_q, c_kv, k_r, merge = jnp.split(h @ w_in, pts, axis=-1)
    kv = kv_a.reshape(B, T, 6, G, D)
    k3 = rope(kv[:, :, 0::2].reshape(B, T, 3 * G, D), pos).reshape(B, T, 3, G, D)
    q_b = (rms_norm(c_q, g_cq) @ w_uq).reshape(B, T, MLA_HEADS, MLA_NOPE_DIM + MLA_ROPE_DIM)
    q_nope = q_b[..., :MLA_NOPE_DIM]
    return {
        'q_a': rope(q_a.reshape(B, T, NSA_HEADS, D), pos),
        'gate_a': gate_a.reshape(B, T, NSA_HEADS, 3),
        'k_cmp': k3[:, :, 0], 'v_cmp': kv[:, :, 1],
        'k_sel': k3[:, :, 1], 'v_sel': kv[:, :, 3],
        'k_win': k3[:, :, 2], 'v_win': kv[:, :, 5],
        'q_lat': jnp.einsum('bthn,chn->bthc', q_nope, w_uk),
        'q_rope': rope(q_b[..., MLA_NOPE_DIM:], pos),
        'c_kv': rms_norm(c_kv, g_ckv),
        'k_r': rope(k_r[:, :, None, :], pos)[:, :, 0],
        'merge': merge,
    }


def mix_block(qn, gn, q_pos, kcm, vcm, gather_sel, kw, vw, kw_pos, q_lat, q_rope, ckv, kr, k_pos):
    f32 = jnp.float32
    B, T = qn.shape[:2]
    G, R, D, BL = NSA_GROUPS, NSA_REP, NSA_HEAD_DIM, NSA_BLOCK
    scale_a = NSA_HEAD_DIM ** -0.5
    q = qn.reshape(B, T, G, R, D)
    nb = kcm.shape[1]
    blk = jnp.arange(nb, dtype=jnp.int32)[None, :]
    tq = q_pos[:, None]
    s = jnp.einsum('btgrd,bngd->btgrn', q, kcm, preferred_element_type=f32) * scale_a
    complete = (blk + 1) * BL - 1 <= tq
    p_cmp = masked_softmax(s, complete[None, :, None, None, :])
    o_cmp = jnp.einsum('btgrn,bngd->btgrd', p_cmp.astype(vcm.dtype), vcm)
    imp = jnp.sum(p_cmp, axis=3)
    forced = (blk == tq // BL) | (blk == 0)
    imp = jnp.where(forced[None, :, None, :], jnp.inf, jnp.where(complete[None, :, None, :], imp, -jnp.inf))
    n_sel = min(NSA_TOP_K, nb)
    _, idx = lax.top_k(imp, n_sel)
    ks, vs = gather_sel(idx)
    sel_pos = idx[..., None] * BL + jnp.arange(BL, dtype=jnp.int32)
    s = jnp.einsum('btgrd,btgkld->btgrkl', q, ks, preferred_element_type=f32) * scale_a
    sel_mask = (sel_pos <= q_pos[None, :, None, None, None]).reshape(B, T, G, 1, n_sel * BL)
    p = masked_softmax(s.reshape(B, T, G, R, n_sel * BL), sel_mask).reshape(B, T, G, R, n_sel, BL)
    o_sel = jnp.einsum('btgrkl,btgkld->btgrd', p.astype(vs.dtype), vs)
    s = jnp.einsum('btgrd,blgd->btgrl', q, kw, preferred_element_type=f32) * scale_a
    kp = kw_pos[None, :]
    win_mask = (kp <= tq) & (kp > tq - NSA_WINDOW) & (kp >= 0)
    p = masked_softmax(s, win_mask[None, :, None, None, :])
    o_win = jnp.einsum('btgrl,blgd->btgrd', p.astype(vw.dtype), vw)
    gate = jax.nn.sigmoid(gn.astype(f32)).astype(q.dtype).reshape(B, T, G, R, 3)
    o_nsa = (gate[..., 0:1] * o_cmp + gate[..., 1:2] * o_sel + gate[..., 2:3] * o_win).reshape(B, T, NSA_HEADS * D)
    scale_b = (MLA_NOPE_DIM + MLA_ROPE_DIM) ** -0.5
    s = (jnp.einsum('bthc,bsc->bths', q_lat, ckv, preferred_element_type=f32)
         + jnp.einsum('bthr,bsr->bths', q_rope, kr, preferred_element_type=f32)) * scale_b
    causal = k_pos[None, :] <= tq
    p = masked_softmax(s, causal[None, :, None, :])
    o_lat = jnp.einsum('bths,bsc->bthc', p.astype(ckv.dtype), ckv)
    return o_nsa, o_lat


def attend_prompt(pr, wlen):
    B, S = pr['q_a'].shape[:2]
    G = NSA_GROUPS
    nb = -(-S // NSA_BLOCK)
    kcm, vcm = block_means(pr['k_cmp'], nb), block_means(pr['v_cmp'], nb)
    ksb, vsb = to_blocks(pr['k_sel'], nb), to_blocks(pr['v_sel'], nb)
    b_ix = jnp.arange(B)[:, None, None, None]
    g_ix = jnp.arange(G)[None, None, :, None]

    def gather_sel(idx):
        return ksb[b_ix, g_ix, idx], vsb[b_ix, g_ix, idx]

    pad_w = [(0, 0), (NSA_WINDOW, 0), (0, 0), (0, 0)]
    kw_pad = jnp.pad(pr['k_win'], pad_w)
    vw_pad = jnp.pad(pr['v_win'], pad_w)
    k_pos = jnp.arange(S, dtype=jnp.int32)
    nqc = S // Q_BLOCK

    def chunks(a):
        return jnp.moveaxis(a.reshape(B, nqc, Q_BLOCK, *a.shape[2:]), 1, 0)

    def unchunk(o):
        return jnp.moveaxis(o, 0, 1).reshape(B, S, *o.shape[3:])

    def body(xs):
        c, qn, gn, ql, qr = xs
        start = c * Q_BLOCK
        q_pos = start + jnp.arange(Q_BLOCK, dtype=jnp.int32)
        band = Q_BLOCK + NSA_WINDOW
        kw = lax.dynamic_slice_in_dim(kw_pad, start, band, axis=1)
        vw = lax.dynamic_slice_in_dim(vw_pad, start, band, axis=1)
        kw_pos = start - NSA_WINDOW + jnp.arange(band, dtype=jnp.int32)
        return mix_block(qn, gn, q_pos, kcm, vcm, gather_sel, kw, vw, kw_pos, ql, qr, pr['c_kv'], pr['k_r'], k_pos)

    o_nsa, o_lat = lax.map(body, (jnp.arange(nqc, dtype=jnp.int32), chunks(pr['q_a']), chunks(pr['gate_a']),
                                  chunks(pr['q_lat']), chunks(pr['q_rope'])))
    front = [(0, 0), (max(0, wlen - S), 0), (0, 0), (0, 0)]
    kw_last = jnp.pad(pr['k_win'], front)[:, -wlen:]
    vw_last = jnp.pad(pr['v_win'], front)[:, -wlen:]
    return unchunk(o_nsa), unchunk(o_lat), jnp.stack([kw_last, vw_last], axis=2)


def attend_sample(pr, l, past_len, cache_cmp, cache_sel, cache_mla, win_state, page_table):
    DB, T = pr['q_a'].shape[:2]
    G, D, BL = NSA_GROUPS, NSA_HEAD_DIM, NSA_BLOCK
    q_pos = past_len + jnp.arange(T, dtype=jnp.int32)
    nb_past = past_len // BL
    nb_new = -(-T // BL)
    past_cmp = cache_cmp[page_table, :, l].reshape(DB, past_len, 2, G, D)
    kcm = jnp.concatenate([block_means(past_cmp[:, :, 0], nb_past), block_means(pr['k_cmp'], nb_new)], axis=1)
    vcm = jnp.concatenate([block_means(past_cmp[:, :, 1], nb_past), block_means(pr['v_cmp'], nb_new)], axis=1)
    ksb_new, vsb_new = to_blocks(pr['k_sel'], nb_new), to_blocks(pr['v_sel'], nb_new)
    b_ix = jnp.arange(DB)[:, None, None, None]
    g_ix = jnp.arange(G)[None, None, :, None]
    bpp = PAGE_SIZE // BL

    def gather_sel(idx):
        is_past = (idx < nb_past)[..., None, None]
        jp = jnp.minimum(idx, nb_past - 1)
        phys = page_table[b_ix, jp // bpp][..., None]
        rows = (jp % bpp)[..., None] * BL + jnp.arange(BL, dtype=jnp.int32)
        gg = g_ix[..., None]
        kp = cache_sel[phys, rows, l, 0, gg]
        vp = cache_sel[phys, rows, l, 1, gg]
        jn = jnp.clip(idx - nb_past, 0, nb_new - 1)
        return (jnp.where(is_past, kp, ksb_new[b_ix, g_ix, jn]),
                jnp.where(is_past, vp, vsb_new[b_ix, g_ix, jn]))

    win = win_state[l]
    wlen = win.shape[1]
    kw = jnp.concatenate([win[:, :, 0], pr['k_win']], axis=1)
    vw = jnp.concatenate([win[:, :, 1], pr['v_win']], axis=1)
    kw_pos = jnp.concatenate([past_len - wlen + jnp.arange(wlen, dtype=jnp.int32), q_pos])
    past_lat = cache_mla[page_table, :, l].reshape(DB, past_len, MLA_LAT_W)
    ckv = jnp.concatenate([past_lat[..., :MLA_KV_RANK], pr['c_kv']], axis=1)
    kr = jnp.concatenate([past_lat[..., MLA_KV_RANK:], pr['k_r']], axis=1)
    k_pos = jnp.arange(past_len + T, dtype=jnp.int32)
    o_nsa, o_lat = mix_block(pr['q_a'], pr['gate_a'], q_pos, kcm, vcm, gather_sel, kw, vw, kw_pos,
                             pr['q_lat'], pr['q_rope'], ckv, kr, k_pos)
    return o_nsa, o_lat, jnp.stack([kw[:, -wlen:], vw[:, -wlen:]], axis=2)


def merge_branches(o_nsa, o_lat, merge, w_uv, w_o_nsa, w_o_mla, w_out):
    B, T = o_nsa.shape[:2]
    o_mla = jnp.einsum('bthc,chv->bthv', o_lat, w_uv).reshape(B, T, MLA_HEADS * MLA_V_DIM)
    g_a, g_b = jnp.split(jax.nn.sigmoid(merge), 2, axis=-1)
    return (g_a * (o_nsa @ w_o_nsa) + g_b * (o_mla @ w_o_mla)) @ w_out


def run_trunk(x, pos, attend, params):
    (g_ffn_a, w13_a, w2_a, g_mix, w_in, g_cq, w_uq, g_ckv, w_uk, w_uv,
     w_o_nsa, w_o_mla, w_out, g_ffn_b, w13_b, w2_b, g_final) = params
    cmp_rows, sel_rows, lat_rows, win_rows = [], [], [], []
    for l in range(DEPTH):
        x = x + 0.5 * swiglu(rms_norm(x, g_ffn_a[l]), w13_a[l], w2_a[l])
        pr = project(rms_norm(x, g_mix[l]), pos, w_in[l], g_cq[l], w_uq[l], g_ckv[l], w_uk[l])
        o_nsa, o_lat, win = attend(l, pr)
        x = x + merge_branches(o_nsa, o_lat, pr['merge'], w_uv[l], w_o_nsa[l], w_o_mla[l], w_out[l])
        x = x + 0.5 * swiglu(rms_norm(x, g_ffn_b[l]), w13_b[l], w2_b[l])
        cmp_rows.append(jnp.stack([pr['k_cmp'], pr['v_cmp']], axis=2))
        sel_rows.append(jnp.stack([pr['k_sel'], pr['v_sel']], axis=2))
        lat_rows.append(jnp.concatenate([pr['c_kv'], pr['k_r']], axis=-1))
        win_rows.append(win)
    return (rms_norm(x, g_final), jnp.stack(cmp_rows, axis=2), jnp.stack(sel_rows, axis=2),
            jnp.stack(lat_rows, axis=2), jnp.stack(win_rows, axis=0))


def setup_inputs(seed: int = 0) -> dict:
    key = jax.random.key(seed)
    ks = jax.random.split(key, 32)
    f32 = jnp.float32
    n_pages = PAST_LEN // PAGE_SIZE
    n_used = DEC_BATCH * n_pages
    n_pool = n_used + max(1, n_used // 4)
    win_len = min(NSA_WINDOW, PAST_LEN)

    def nrm(k, shape, scale=1.0):
        return jax.random.normal(k, shape, f32) * scale

    def gain(k, shape):
        return 1.0 + 0.02 * jax.random.normal(k, shape, f32)

    page_table = jax.random.permutation(ks[0], n_pool)[:n_used].reshape(DEC_BATCH, n_pages).astype(jnp.int32)
    return {
        'x_prompt': nrm(ks[1], (BATCH, SEQ, D_MODEL)),
        'x_sample': nrm(ks[2], (DEC_BATCH, DEC_SEQ, D_MODEL)),
        'cache_nsa_cmp': nrm(ks[3], (n_pool, PAGE_SIZE, DEPTH, 2, NSA_GROUPS, NSA_HEAD_DIM)),
        'cache_nsa_sel': nrm(ks[4], (n_pool, PAGE_SIZE, DEPTH, 2, NSA_GROUPS, NSA_HEAD_DIM)),
        'cache_mla': nrm(ks[5], (n_pool, PAGE_SIZE, DEPTH, MLA_LAT_W)),
        'state_nsa_win': nrm(ks[6], (DEPTH, DEC_BATCH, win_len, 2, NSA_GROUPS, NSA_HEAD_DIM)),
        'page_table': page_table,
        'g_ffn_a': gain(ks[7], (DEPTH, D_MODEL)),
        'w13_a': nrm(ks[8], (DEPTH, D_MODEL, 2 * D_FF), D_MODEL ** -0.5),
        'w2_a': nrm(ks[9], (DEPTH, D_FF, D_MODEL), D_FF ** -0.5),
        'g_mix': gain(ks[10], (DEPTH, D_MODEL)),
        'w_in': nrm(ks[11], (DEPTH, D_MODEL, IN_WIDTH), D_MODEL ** -0.5),
        'g_cq': gain(ks[12], (DEPTH, MLA_Q_RANK)),
        'w_uq': nrm(ks[13], (DEPTH, MLA_Q_RANK, MLA_HEADS * (MLA_NOPE_DIM + MLA_ROPE_DIM)), MLA_Q_RANK ** -0.5),
        'g_ckv': gain(ks[14], (DEPTH, MLA_KV_RANK)),
        'w_uk': nrm(ks[15], (DEPTH, MLA_KV_RANK, MLA_HEADS, MLA_NOPE_DIM), MLA_KV_RANK ** -0.5),
        'w_uv': nrm(ks[16], (DEPTH, MLA_KV_RANK, MLA_HEADS, MLA_V_DIM), MLA_KV_RANK ** -0.5),
        'w_o_nsa': nrm(ks[17], (DEPTH, NSA_HEADS * NSA_HEAD_DIM, D_MODEL), (NSA_HEADS * NSA_HEAD_DIM) ** -0.5),
        'w_o_mla': nrm(ks[18], (DEPTH, MLA_HEADS * MLA_V_DIM, D_MODEL), (MLA_HEADS * MLA_V_DIM) ** -0.5),
        'w_out': nrm(ks[19], (DEPTH, D_MODEL, D_MODEL), D_MODEL ** -0.5),
        'g_ffn_b': gain(ks[20], (DEPTH, D_MODEL)),
        'w13_b': nrm(ks[21], (DEPTH, D_MODEL, 2 * D_FF), D_MODEL ** -0.5),
        'w2_b': nrm(ks[22], (DEPTH, D_FF, D_MODEL), D_FF ** -0.5),
        'g_final': gain(ks[23], (D_MODEL,)),
    }


def reference(x_prompt, x_sample, cache_nsa_cmp, cache_nsa_sel, cache_mla, state_nsa_win, page_table,
              g_ffn_a, w13_a, w2_a, g_mix, w_in, g_cq, w_uq, g_ckv, w_uk, w_uv,
              w_o_nsa, w_o_mla, w_out, g_ffn_b, w13_b, w2_b, g_final):
    params = (g_ffn_a, w13_a, w2_a, g_mix, w_in, g_cq, w_uq, g_ckv, w_uk, w_uv,
              w_o_nsa, w_o_mla, w_out, g_ffn_b, w13_b, w2_b, g_final)
    wlen = state_nsa_win.shape[2]
    pos_p = jnp.arange(x_prompt.shape[1], dtype=jnp.int32)
    y_prompt, cmp_p, sel_p, mla_p, win_p = run_trunk(
        x_prompt, pos_p, lambda l, pr: attend_prompt(pr, wlen), params)
    past_len = page_table.shape[1] * PAGE_SIZE
    pos_s = past_len + jnp.arange(x_sample.shape[1], dtype=jnp.int32)
    y_sample, cmp_s, sel_s, mla_s, win_s = run_trunk(
        x_sample, pos_s,
        lambda l, pr: attend_sample(pr, l, past_len, cache_nsa_cmp, cache_nsa_sel, cache_mla, state_nsa_win, page_table),
        params)
    return (y_prompt, y_sample, cmp_p, cmp_s, sel_p, sel_s, mla_p, mla_s, win_p, win_s)
```

```python
import functools
import math

import jax
import jax.numpy as jnp
import numpy as np
from jax import lax
from jax.experimental import pallas as pl
from jax.experimental.pallas import tpu as pltpu

f32 = jnp.float32
bf16 = jnp.bfloat16

NSA_HEADS = 8
NSA_GROUPS = 2
NSA_HEAD_DIM = 64
NSA_REP = NSA_HEADS // NSA_GROUPS
NSA_BLOCK = 64
NSA_TOP_K = 16
NSA_WINDOW = 512
MLA_HEADS = 8
MLA_Q_RANK = 256
MLA_KV_RANK = 128
MLA_NOPE_DIM = 64
MLA_ROPE_DIM = 32
MLA_V_DIM = 64
MLA_LAT_W = MLA_KV_RANK + MLA_ROPE_DIM
ROPE_THETA = 10000.0
RMS_EPS = 1e-6
PAGE_SIZE = 128
NSA_KV_W = NSA_GROUPS * NSA_HEAD_DIM
NSA_ROW_W = 2 * NSA_KV_W

LANES = 128
VMEM_LIMIT_BYTES = 56 * 1024 * 1024

NEG_BIG = -1e30


def _cparams(*sem):
    return pltpu.CompilerParams(dimension_semantics=sem, vmem_limit_bytes=VMEM_LIMIT_BYTES)


def _rms(x, g):
    ms = jnp.mean(x * x, axis=-1, keepdims=True)
    return x * lax.rsqrt(ms + RMS_EPS) * g


def _dot(a, b):
    return jnp.dot(a, b, preferred_element_type=f32)


def _dot_nt(a, b):
    return lax.dot_general(a, b, (((1,), (1,)), ((), ())), preferred_element_type=f32)


def _ffn_body(x_ref, g_ref, w1_ref, w3_ref, w2_ref, gf_ref, o_ref, h_sc, acc_sc, *, final_norm):
    j = pl.program_id(1)

    @pl.when(j == 0)
    def _():
        h_sc[...] = _rms(x_ref[...], g_ref[...]).astype(bf16)
        acc_sc[...] = jnp.zeros_like(acc_sc)

    h = h_sc[...]
    a = _dot(h, w1_ref[...])
    b = _dot(h, w3_ref[...])
    u = (a * jax.nn.sigmoid(a) * b).astype(bf16)
    acc_sc[...] += _dot(u, w2_ref[...])

    @pl.when(j == pl.num_programs(1) - 1)
    def _():
        y = x_ref[...] + 0.5 * acc_sc[...]
        if final_norm:
            y = _rms(y, gf_ref[...])
        o_ref[...] = y


def _ffn(x, g, w13, w2, gf, *, final_norm, tm, tf):
    n, d = x.shape
    dff = w2.shape[0]
    nf = dff // tf
    return pl.pallas_call(
        functools.partial(_ffn_body, final_norm=final_norm),
        out_shape=jax.ShapeDtypeStruct((n, d), f32),
        grid=(n // tm, nf),
        in_specs=[
            pl.BlockSpec((tm, d), lambda i, j: (i, 0)),
            pl.BlockSpec((1, d), lambda i, j: (0, 0)),
            pl.BlockSpec((d, tf), lambda i, j: (0, j)),
            pl.BlockSpec((d, tf), lambda i, j: (0, nf + j)),
            pl.BlockSpec((tf, d), lambda i, j: (j, 0)),
            pl.BlockSpec((1, d), lambda i, j: (0, 0)),
        ],
        out_specs=pl.BlockSpec((tm, d), lambda i, j: (i, 0)),
        scratch_shapes=[pltpu.VMEM((tm, d), bf16), pltpu.VMEM((tm, d), f32)],
        compiler_params=_cparams("arbitrary", "arbitrary"),
        name="ffn",
    )(x, g, w13, w13, w2, gf)


def _rope_lanes(x, cos, sin_signed, half):
    w = x.shape[-1]
    lane = lax.broadcasted_iota(jnp.int32, x.shape, x.ndim - 1)
    first = (lane % (2 * half)) < half
    rot = jnp.where(first, pltpu.roll(x, w - half, x.ndim - 1), pltpu.roll(x, half, x.ndim - 1))
    return x * cos + rot * sin_signed


def _rope_rows(x, cos, sin, half):
    x1, x2 = x[:half], x[half:]
    return jnp.concatenate([x1 * cos - x2 * sin, x2 * cos + x1 * sin], axis=0)


def _proj_body(x_ref, g_ref, ws_ref, wt_ref, gcq_ref, wuq_ref, wuk_ref, gckv_ref,
               ca_ref, sa_ref, cb_ref, sb_ref, cat_ref, sat_ref, cbt_ref, sbt_ref,
               qa_ref, gate_ref, qm_ref, cmp_ref, sel_ref, win_ref, lat_ref):
    h = _rms(x_ref[...], g_ref[...]).astype(bf16)
    ps = _dot(h, ws_ref[...])
    pt = _dot_nt(wt_ref[...], h)
    nq = NSA_HEADS * NSA_HEAD_DIM
    scale_a = NSA_HEAD_DIM ** -0.5
    ca, sa = ca_ref[...], sa_ref[...]
    for c in range(nq // LANES):
        q = _rope_lanes(ps[:, c * LANES:(c + 1) * LANES], ca, sa, NSA_HEAD_DIM // 2) * scale_a
        per = LANES // NSA_HEAD_DIM
        for r in range(per):
            qa_ref[c * per + r] = q[:, r * NSA_HEAD_DIM:(r + 1) * NSA_HEAD_DIM].astype(qa_ref.dtype)
    gate_ref[...] = jax.nn.sigmoid(ps[:, nq + MLA_Q_RANK:nq + MLA_Q_RANK + LANES])
    cqn = _rms(ps[:, nq:nq + MLA_Q_RANK], gcq_ref[...]).astype(bf16)
    qb = _dot(cqn, wuq_ref[...])
    n_nope = MLA_HEADS * MLA_NOPE_DIM
    scale_b = (MLA_NOPE_DIM + MLA_ROPE_DIM) ** -0.5
    q_lat = _dot(qb[:, :n_nope].astype(bf16), wuk_ref[...]) * scale_b
    cb, sb = cb_ref[...], sb_ref[...]
    per = LANES // MLA_ROPE_DIM
    for c in range(MLA_HEADS // per):
        lo = n_nope + c * LANES
        q_rope = _rope_lanes(qb[:, lo:lo + LANES], cb, sb, MLA_ROPE_DIM // 2) * scale_b
        for r in range(per):
            qm_ref[c * per + r, :, MLA_KV_RANK:MLA_LAT_W] = (
                q_rope[:, r * MLA_ROPE_DIM:(r + 1) * MLA_ROPE_DIM].astype(qm_ref.dtype))
    for hh in range(MLA_HEADS):
        qm_ref[hh, :, 0:MLA_KV_RANK] = q_lat[:, hh * MLA_KV_RANK:(hh + 1) * MLA_KV_RANK].astype(qm_ref.dtype)
    cat, sat = cat_ref[...], sat_ref[...]
    for br, o_ref in enumerate((cmp_ref, sel_ref, win_ref)):
        base = br * NSA_ROW_W
        for gi in range(NSA_GROUPS):
            lo = base + gi * NSA_HEAD_DIM
            o_ref[0, gi * NSA_HEAD_DIM:(gi + 1) * NSA_HEAD_DIM, :] = _rope_rows(
                pt[lo:lo + NSA_HEAD_DIM], cat, sat, NSA_HEAD_DIM // 2)
        o_ref[0, NSA_KV_W:NSA_ROW_W, :] = pt[base + NSA_KV_W:base + NSA_ROW_W]
    base = 3 * NSA_ROW_W
    ckv = pt[base:base + MLA_KV_RANK]
    ms = jnp.mean(ckv * ckv, axis=0, keepdims=True)
    lat_ref[0, 0:MLA_KV_RANK, :] = ckv * lax.rsqrt(ms + RMS_EPS) * gckv_ref[...]
    lat_ref[0, MLA_KV_RANK:MLA_LAT_W, :] = _rope_rows(
        pt[base + MLA_KV_RANK:base + MLA_LAT_W], cbt_ref[...], sbt_ref[...], MLA_ROPE_DIM // 2)


def _proj(x, seq_len, wl, tabs, *, tm, q_dtype):
    n, d = x.shape
    nper = seq_len // tm
    n_seq = n // seq_len
    tok = lambda i: (i, 0)
    per = lambda i: (i % nper, 0)
    per_t = lambda i: (0, i % nper)
    const = lambda i: (0, 0)
    kv_map = lambda i: (i // nper, 0, i % nper)
    ws, wt, wuq, wuk = wl["ws"], wl["wt"], wl["wuq"], wl["wuk_bd"]
    out_shape = (
        jax.ShapeDtypeStruct((NSA_HEADS, n, NSA_HEAD_DIM), q_dtype),
        jax.ShapeDtypeStruct((n, LANES), f32),
        jax.ShapeDtypeStruct((MLA_HEADS, n, MLA_LAT_W), q_dtype),
        jax.ShapeDtypeStruct((n_seq, NSA_ROW_W, seq_len), f32),
        jax.ShapeDtypeStruct((n_seq, NSA_ROW_W, seq_len), f32),
        jax.ShapeDtypeStruct((n_seq, NSA_ROW_W, seq_len), f32),
        jax.ShapeDtypeStruct((n_seq, MLA_LAT_W, seq_len), f32),
    )
    out_specs = (
        pl.BlockSpec((NSA_HEADS, tm, NSA_HEAD_DIM), lambda i: (0, i, 0)),
        pl.BlockSpec((tm, LANES), tok),
        pl.BlockSpec((MLA_HEADS, tm, MLA_LAT_W), lambda i: (0, i, 0)),
        pl.BlockSpec((1, NSA_ROW_W, tm), kv_map),
        pl.BlockSpec((1, NSA_ROW_W, tm), kv_map),
        pl.BlockSpec((1, NSA_ROW_W, tm), kv_map),
        pl.BlockSpec((1, MLA_LAT_W, tm), kv_map),
    )
    in_specs = [
        pl.BlockSpec((tm, d), tok),
        pl.BlockSpec((1, d), const),
        pl.BlockSpec(ws.shape, const),
        pl.BlockSpec(wt.shape, const),
        pl.BlockSpec((1, MLA_Q_RANK), const),
        pl.BlockSpec(wuq.shape, const),
        pl.BlockSpec(wuk.shape, const),
        pl.BlockSpec((MLA_KV_RANK, 1), const),
        pl.BlockSpec((tm, LANES), per), pl.BlockSpec((tm, LANES), per),
        pl.BlockSpec((tm, LANES), per), pl.BlockSpec((tm, LANES), per),
        pl.BlockSpec((NSA_HEAD_DIM // 2, tm), per_t), pl.BlockSpec((NSA_HEAD_DIM // 2, tm), per_t),
        pl.BlockSpec((MLA_ROPE_DIM // 2, tm), per_t), pl.BlockSpec((MLA_ROPE_DIM // 2, tm), per_t),
    ]
    return pl.pallas_call(
        _proj_body,
        out_shape=out_shape,
        grid=(n // tm,),
        in_specs=in_specs,
        out_specs=out_specs,
        compiler_params=_cparams("arbitrary"),
        name="proj",
    )(x, wl["g_mix"], ws, wt, wl["g_cq"], wuq, wuk, wl["g_ckv"],
      tabs["ca"], tabs["sa"], tabs["cb"], tabs["sb"], tabs["cat"], tabs["sat"], tabs["cbt"], tabs["sbt"])


def _merge_body(x_ref, g_ref, wm_ref, on_ref, ol_ref, won_ref, wuv_ref, wom_ref, wout_ref, o_ref):
    x = x_ref[...]
    d = x.shape[-1]
    h = _rms(x, g_ref[...]).astype(bf16)
    mg = _dot(h, wm_ref[...])
    ga = jax.nn.sigmoid(mg[:, :d])
    gb = jax.nn.sigmoid(mg[:, d:])
    a = _dot(on_ref[...].astype(bf16), won_ref[...])
    om = _dot(ol_ref[...].astype(bf16), wuv_ref[...]).astype(bf16)
    b = _dot(om, wom_ref[...])
    m = (ga * a + gb * b).astype(bf16)
    o_ref[...] = x + _dot(m, wout_ref[...])


def _merge(x, o_nsa, o_lat, wl, *, tm):
    n, d = x.shape
    tok = lambda i: (i, 0)
    const = lambda i: (0, 0)
    ws = [wl["wm"], wl["w_o_nsa"], wl["wuv_bd"], wl["w_o_mla"], wl["w_out"]]
    return pl.pallas_call(
        _merge_body,
        out_shape=jax.ShapeDtypeStruct((n, d), f32),
        grid=(n // tm,),
        in_specs=[
            pl.BlockSpec((tm, d), tok),
            pl.BlockSpec((1, d), const),
            pl.BlockSpec(ws[0].shape, const),
            pl.BlockSpec((tm, o_nsa.shape[1]), tok),
            pl.BlockSpec((tm, o_lat.shape[1]), tok),
            pl.BlockSpec(ws[1].shape, const),
            pl.BlockSpec(ws[2].shape, const),
            pl.BlockSpec(ws[3].shape, const),
            pl.BlockSpec(ws[4].shape, const),
        ],
        out_specs=pl.BlockSpec((tm, d), tok),
        compiler_params=_cparams("arbitrary"),
        name="merge",
    )(x, wl["g_mix"], ws[0], o_nsa, o_lat, ws[1], ws[2], ws[3], ws[4])


def _masked_softmax_rows(s, mask):
    sm = jnp.where(mask, s, -jnp.inf)
    m = jnp.max(sm, axis=-1, keepdims=True)
    m = jnp.where(m == -jnp.inf, 0.0, m)
    e = jnp.where(mask, jnp.exp(s - m), 0.0)
    return e / jnp.maximum(jnp.sum(e, axis=-1, keepdims=True), 1e-30)


def _online_step(carry, s, mask, v_t):
    m_i, l_i, acc = carry
    sm = jnp.where(mask, s, NEG_BIG)
    m_new = jnp.maximum(m_i, jnp.max(sm, axis=-1, keepdims=True))
    p = jnp.where(mask, jnp.exp(s - m_new), 0.0)
    alpha = jnp.exp(m_i - m_new)
    l_new = alpha * l_i + jnp.sum(p, axis=-1, keepdims=True)
    acc_new = alpha * acc + _dot_nt(p.astype(bf16), v_t)
    return m_new, l_new, acc_new


def _online_init(rows, dv):
    return (jnp.full((rows, 1), NEG_BIG, f32), jnp.zeros((rows, 1), f32), jnp.zeros((rows, dv), f32))


def _block_select(imp, tpos, complete, n_blocks):
    blk = lax.broadcasted_iota(jnp.int32, imp.shape, 1)
    forced = (blk == tpos // NSA_BLOCK) | (blk == 0)
    v = jnp.where(forced, jnp.inf, jnp.where(complete, imp, -jnp.inf))
    rank = jnp.zeros(imp.shape, jnp.int32)
    for i in range(n_blocks):
        col = v[:, i:i + 1]
        beats = (col > v) | ((col == v) & (blk > i))
        rank = rank + beats.astype(jnp.int32)
    return rank < min(NSA_TOP_K, n_blocks)


def _nsa_prompt_body(qa_ref, gate_ref, cmp_ref, sel_ref, win_ref, o_ref, kcm_sc, sel_sc, win_sc,
                     *, tq, kc, seq_len):
    c = pl.program_id(1)
    nb = seq_len // NSA_BLOCK
    hd = NSA_HEAD_DIM

    @pl.when(c == 0)
    def _():
        x = cmp_ref[0]
        row = lax.broadcasted_iota(jnp.int32, (seq_len, nb), 0)
        col = lax.broadcasted_iota(jnp.int32, (seq_len, nb), 1)
        avg = jnp.where(row // NSA_BLOCK == col, 1.0 / NSA_BLOCK, 0.0).astype(bf16)
        hi = x.astype(bf16)
        lo = (x - hi.astype(f32)).astype(bf16)
        kcm_sc[...] = _dot(hi, avg) + _dot(lo, avg)
        for ch in range(seq_len // kc):
            sel_sc[ch] = sel_ref[0, :, ch * kc:(ch + 1) * kc].astype(bf16)
        for ch in range(seq_len // LANES):
            win_sc[ch] = win_ref[0, :, ch * LANES:(ch + 1) * LANES].astype(bf16)

    t0 = c * tq
    tpos = t0 + lax.broadcasted_iota(jnp.int32, (tq, 1), 0)
    blk = lax.broadcasted_iota(jnp.int32, (1, nb), 1)
    complete = (blk + 1) * NSA_BLOCK - 1 <= tpos
    gates = gate_ref[...]
    n_sel_chunks = (t0 + tq + kc - 1) // kc
    rows = NSA_REP * tq

    for g in range(NSA_GROUPS):
        qg = qa_ref[g * NSA_REP:(g + 1) * NSA_REP].reshape(rows, hd)
        kc_t = kcm_sc[g * hd:(g + 1) * hd, :].astype(bf16)
        vc_t = kcm_sc[NSA_KV_W + g * hd:NSA_KV_W + (g + 1) * hd, :].astype(bf16)
        s = _dot(qg, kc_t).reshape(NSA_REP, tq, nb)
        p_cmp = _masked_softmax_rows(s, complete[None])
        o_cmp = _dot_nt(p_cmp.reshape(rows, nb).astype(bf16), vc_t)
        chosen = _block_select(jnp.sum(p_cmp, axis=0), tpos, complete, nb)
        chosen_b = chosen.astype(bf16)

        def sel_step(ch, carry):
            k_t = sel_sc[ch, g * hd:(g + 1) * hd, :]
            v_t = sel_sc[ch, NSA_KV_W + g * hd:NSA_KV_W + (g + 1) * hd, :]
            s = _dot(qg, k_t).reshape(NSA_REP, tq, kc)
            kpos = ch * kc + lax.broadcasted_iota(jnp.int32, (1, kc), 1)
            eb = lax.broadcasted_iota(jnp.int32, (nb, kc), 0)
            ek = ch * kc + lax.broadcasted_iota(jnp.int32, (nb, kc), 1)
            expand = jnp.where(ek // NSA_BLOCK == eb, 1.0, 0.0).astype(bf16)
            mask = (_dot(chosen_b, expand) > 0.5) & (kpos <= tpos)
            mask = jnp.broadcast_to(mask[None], (NSA_REP, tq, kc)).reshape(rows, kc)
            return _online_step(carry, s.reshape(rows, kc), mask, v_t)

        _, l_s, acc_s = lax.fori_loop(0, n_sel_chunks, sel_step, _online_init(rows, hd))
        o_sel = acc_s / jnp.maximum(l_s, 1e-30)

        n_band = NSA_WINDOW // LANES + tq // LANES
        s_band, v_band, m_band = [], [], []
        for j in range(n_band):
            idx = c - (n_band - 1) + j
            idc = jnp.maximum(idx, 0)
            k_t = win_sc[idc, g * hd:(g + 1) * hd, :]
            v_band.append(win_sc[idc, NSA_KV_W + g * hd:NSA_KV_W + (g + 1) * hd, :])
            kpos = idx * LANES + lax.broadcasted_iota(jnp.int32, (1, LANES), 1)
            mask = (kpos >= 0) & (kpos <= tpos) & (kpos > tpos - NSA_WINDOW)
            m_band.append(jnp.broadcast_to(mask[None], (NSA_REP, tq, LANES)).reshape(rows, LANES))
            s_band.append(_dot(qg, k_t))
        mx = functools.reduce(jnp.maximum, [
            jnp.max(jnp.where(mk, sj, NEG_BIG), axis=-1, keepdims=True) for sj, mk in zip(s_band, m_band)])
        l_w = jnp.zeros((rows, 1), f32)
        acc_w = jnp.zeros((rows, hd), f32)
        for sj, mk, vj in zip(s_band, m_band, v_band):
            e = jnp.where(mk, jnp.exp(sj - mx), 0.0)
            l_w = l_w + jnp.sum(e, axis=-1, keepdims=True)
            acc_w = acc_w + _dot_nt(e.astype(bf16), vj)
        o_win = acc_w / jnp.maximum(l_w, 1e-30)

        for r in range(NSA_REP):
            hh = g * NSA_REP + r
            sl = slice(r * tq, (r + 1) * tq)
            o = (gates[:, hh:hh + 1] * o_cmp[sl]
                 + gates[:, NSA_HEADS + hh:NSA_HEADS + hh + 1] * o_sel[sl]
                 + gates[:, 2 * NSA_HEADS + hh:2 * NSA_HEADS + hh + 1] * o_win[sl])
            o_ref[:, hh * hd:(hh + 1) * hd] = o.astype(o_ref.dtype)


def _nsa_prompt(qa, gates, cmp_t, sel_t, win_t, *, tq, kc):
    n_seq, _, seq_len = cmp_t.shape
    n = n_seq * seq_len
    nc = seq_len // tq
    nb = seq_len // NSA_BLOCK
    kv_spec = pl.BlockSpec((1, NSA_ROW_W, seq_len), lambda b, c: (b, 0, 0))
    return pl.pallas_call(
        functools.partial(_nsa_prompt_body, tq=tq, kc=kc, seq_len=seq_len),
        out_shape=jax.ShapeDtypeStruct((n, NSA_HEADS * NSA_HEAD_DIM), bf16),
        grid=(n_seq, nc),
        in_specs=[
            pl.BlockSpec((NSA_HEADS, tq, NSA_HEAD_DIM), lambda b, c: (0, b * nc + c, 0)),
            pl.BlockSpec((tq, LANES), lambda b, c: (b * nc + c, 0)),
            kv_spec, kv_spec, kv_spec,
        ],
        out_specs=pl.BlockSpec((tq, NSA_HEADS * NSA_HEAD_DIM), lambda b, c: (b * nc + c, 0)),
        scratch_shapes=[
            pltpu.VMEM((NSA_ROW_W, nb), f32),
            pltpu.VMEM((seq_len // kc, NSA_ROW_W, kc), bf16),
            pltpu.VMEM((seq_len // LANES, NSA_ROW_W, LANES), bf16),
        ],
        compiler_params=_cparams("arbitrary", "arbitrary"),
        name="nsa_prompt",
    )(qa, gates, cmp_t, sel_t, win_t)


def _mla_prompt_body(qm_ref, lat_ref, o_ref, lat_sc, *, tq, kc, seq_len):
    c = pl.program_id(1)

    @pl.when(c == 0)
    def _():
        for ch in range(seq_len // kc):
            lat_sc[ch] = lat_ref[0, :, ch * kc:(ch + 1) * kc].astype(bf16)

    rows = MLA_HEADS * tq
    q = qm_ref[...].reshape(rows, MLA_LAT_W)
    t0 = c * tq
    tpos = t0 + lax.broadcasted_iota(jnp.int32, (tq, 1), 0)

    def step(ch, carry):
        k_t = lat_sc[ch]
        s = _dot(q, k_t)
        kpos = ch * kc + lax.broadcasted_iota(jnp.int32, (1, kc), 1)
        mask = jnp.broadcast_to((kpos <= tpos)[None], (MLA_HEADS, tq, kc)).reshape(rows, kc)
        return _online_step(carry, s, mask, k_t[:MLA_KV_RANK])

    n_chunks = (t0 + tq + kc - 1) // kc
    _, l_i, acc = lax.fori_loop(0, n_chunks, step, _online_init(rows, MLA_KV_RANK))
    o = acc / jnp.maximum(l_i, 1e-30)
    for hh in range(MLA_HEADS):
        o_ref[:, hh * MLA_KV_RANK:(hh + 1) * MLA_KV_RANK] = o[hh * tq:(hh + 1) * tq].astype(o_ref.dtype)


def _mla_prompt(qm, lat_t, *, tq, kc):
    n_seq, _, seq_len = lat_t.shape
    n = n_seq * seq_len
    nc = seq_len // tq
    return pl.pallas_call(
        functools.partial(_mla_prompt_body, tq=tq, kc=kc, seq_len=seq_len),
        out_shape=jax.ShapeDtypeStruct((n, MLA_HEADS * MLA_KV_RANK), bf16),
        grid=(n_seq, nc),
        in_specs=[
            pl.BlockSpec((MLA_HEADS, tq, MLA_LAT_W), lambda b, c: (0, b * nc + c, 0)),
            pl.BlockSpec((1, MLA_LAT_W, seq_len), lambda b, c: (b, 0, 0)),
        ],
        out_specs=pl.BlockSpec((tq, MLA_HEADS * MLA_KV_RANK), lambda b, c: (b * nc + c, 0)),
        scratch_shapes=[pltpu.VMEM((seq_len // kc, MLA_LAT_W, kc), bf16)],
        compiler_params=_cparams("arbitrary", "arbitrary"),
        name="mla_prompt",
    )(qm, lat_t)


PAGES_PER_CHUNK = 16


def _page_copy(cache_ref, layer, page, buf, slot, pi, sem):
    return pltpu.make_async_copy(cache_ref.at[page, layer], buf.at[slot, :, pl.ds(pi * LANES, LANES)], sem.at[slot])


def _start_chunk(pt_ref, cache_ref, layer, b, chunk, buf, sem):
    slot = chunk % 2
    for pi in range(PAGES_PER_CHUNK):
        page = pt_ref[b, chunk * PAGES_PER_CHUNK + pi]
        _page_copy(cache_ref, layer, page, buf, slot, pi, sem).start()


def _wait_chunk(cache_ref, layer, chunk, buf, sem):
    slot = chunk % 2
    for pi in range(PAGES_PER_CHUNK):
        _page_copy(cache_ref, layer, 0, buf, slot, pi, sem).wait()


def _stream_chunks(pt_ref, cache_ref, layer, b, n_chunks, buf, sem, fn, carry):
    _start_chunk(pt_ref, cache_ref, layer, b, 0, buf, sem)
    for chunk in range(n_chunks):
        if chunk + 1 < n_chunks:
            _start_chunk(pt_ref, cache_ref, layer, b, chunk + 1, buf, sem)
        _wait_chunk(cache_ref, layer, chunk, buf, sem)
        carry = fn(chunk, buf[chunk % 2], carry)
    return carry


def _block_avg_matrix(n_keys):
    nb = n_keys // NSA_BLOCK
    row = lax.broadcasted_iota(jnp.int32, (n_keys, nb), 0)
    col = lax.broadcasted_iota(jnp.int32, (n_keys, nb), 1)
    return jnp.where(row // NSA_BLOCK == col, 1.0 / NSA_BLOCK, 0.0).astype(bf16)


def _block_means(x, avg):
    hi = x.astype(bf16)
    lo = (x - hi.astype(f32)).astype(bf16)
    return _dot(hi, avg) + _dot(lo, avg)


def _cmp_means_body(pt_ref, cache_ref, o_ref, buf, sem, *, layer, n_chunks):
    b = pl.program_id(0)
    ck = PAGES_PER_CHUNK * PAGE_SIZE
    avg = _block_avg_matrix(ck)
    nbc = ck // NSA_BLOCK

    def fn(chunk, slab, carry):
        o_ref[0, :, chunk * nbc:(chunk + 1) * nbc] = _block_means(slab, avg)
        return carry

    _stream_chunks(pt_ref, cache_ref, layer, b, n_chunks, buf, sem, fn, None)


def _cmp_means(page_table, cache_t, layer):
    n_dec, n_pages = page_table.shape
    n_chunks = n_pages // PAGES_PER_CHUNK
    nbp = n_pages * PAGE_SIZE // NSA_BLOCK
    ck = PAGES_PER_CHUNK * PAGE_SIZE
    return pl.pallas_call(
        functools.partial(_cmp_means_body, layer=layer, n_chunks=n_chunks),
        out_shape=jax.ShapeDtypeStruct((n_dec, NSA_ROW_W, nbp), f32),
        grid_spec=pltpu.PrefetchScalarGridSpec(
            num_scalar_prefetch=1,
            grid=(n_dec,),
            in_specs=[pl.BlockSpec(memory_space=pl.ANY)],
            out_specs=pl.BlockSpec((1, NSA_ROW_W, nbp), lambda b, pt: (b, 0, 0)),
            scratch_shapes=[pltpu.VMEM((2, NSA_ROW_W, ck), f32), pltpu.SemaphoreType.DMA((2,))],
        ),
        compiler_params=_cparams("arbitrary"),
        name="cmp_means",
    )(page_table, cache_t)


def _pick_past_blocks(imp, k_past):
    blk = lax.broadcasted_iota(jnp.int32, imp.shape, 1)
    v = jnp.where(blk == 0, jnp.inf, imp)
    chosen = jnp.zeros(imp.shape, jnp.bool_)
    for _ in range(k_past):
        m = jnp.max(v, axis=1, keepdims=True)
        idx = jnp.min(jnp.where(v == m, blk, imp.shape[1]), axis=1, keepdims=True)
        hit = blk == idx
        chosen = chosen | hit
        v = jnp.where(hit, -1.0, v)
    return chosen


def _nsa_sample_body(pt_ref, qa_ref, gate_ref, kcm_ref, cmpn_ref, seln_ref, winn_ref, wst_ref, cache_ref,
                     o_ref, buf, sem, *, layer, n_chunks, past_len, t_new):
    b = pl.program_id(0)
    hd = NSA_HEAD_DIM
    rows = NSA_REP * t_new
    nbp = past_len // NSA_BLOCK
    ck = PAGES_PER_CHUNK * PAGE_SIZE
    nbc = ck // NSA_BLOCK
    wlen = wst_ref.shape[-1]
    tpos = past_len + lax.broadcasted_iota(jnp.int32, (t_new, 1), 0)
    t_row = lax.broadcasted_iota(jnp.int32, (t_new, 1), 0)
    t_col = lax.broadcasted_iota(jnp.int32, (1, t_new), 1)
    causal_new = t_col <= t_row
    gates = gate_ref[...]

    def heads(mask):
        return jnp.broadcast_to(mask[None], (NSA_REP,) + mask.shape).reshape(rows, mask.shape[-1])

    qs, chosen_b, o_cmps = [], [], []
    for g in range(NSA_GROUPS):
        qg = qa_ref[g * NSA_REP:(g + 1) * NSA_REP].reshape(rows, hd).astype(bf16)
        qs.append(qg)
        ks, vs = slice(g * hd, (g + 1) * hd), slice(NSA_KV_W + g * hd, NSA_KV_W + (g + 1) * hd)
        s = _dot(qg, kcm_ref[0, ks, :].astype(bf16))
        cmp_new = jnp.sum(cmpn_ref[0], axis=1, keepdims=True) * (1.0 / NSA_BLOCK)
        s_new = _dot(qg, cmp_new[ks].astype(bf16))
        new_complete = heads((nbp + 1) * NSA_BLOCK - 1 <= tpos)
        m = jnp.maximum(jnp.max(s, axis=-1, keepdims=True), jnp.where(new_complete, s_new, -jnp.inf))
        e = jnp.exp(s - m)
        e_new = jnp.where(new_complete, jnp.exp(s_new - m), 0.0)
        den = jnp.maximum(jnp.sum(e, axis=-1, keepdims=True) + e_new, 1e-30)
        p = e / den
        p_new = e_new / den
        o_cmp = _dot_nt(p.astype(bf16), kcm_ref[0, vs, :].astype(bf16))
        o_cmp = o_cmp + _dot_nt(p_new.astype(bf16), cmp_new[vs].astype(bf16))
        o_cmps.append(o_cmp)
        imp = jnp.sum(p.reshape(NSA_REP, t_new, nbp), axis=0)
        chosen = _pick_past_blocks(imp, min(NSA_TOP_K, nbp + 1) - 1)
        chosen_b.append(chosen.astype(bf16))

    eb = lax.broadcasted_iota(jnp.int32, (nbc, ck), 0)
    ek = lax.broadcasted_iota(jnp.int32, (nbc, ck), 1)
    expand = jnp.where(ek // NSA_BLOCK == eb, 1.0, 0.0).astype(bf16)

    def fn(chunk, slab, carry):
        out = []
        for g in range(NSA_GROUPS):
            k_t = slab[g * hd:(g + 1) * hd].astype(bf16)
            v_t = slab[NSA_KV_W + g * hd:NSA_KV_W + (g + 1) * hd].astype(bf16)
            s = _dot(qs[g], k_t)
            mask = heads(_dot(chosen_b[g][:, chunk * nbc:(chunk + 1) * nbc], expand) > 0.5)
            out.append(_online_step(carry[g], s, mask, v_t))
        return out

    carry = _stream_chunks(pt_ref, cache_ref, layer, b, n_chunks, buf, sem, fn,
                           [_online_init(rows, hd) for _ in range(NSA_GROUPS)])

    wi = lax.broadcasted_iota(jnp.int32, (1, wlen), 1)
    win_mask = wi > t_row + (wlen - NSA_WINDOW)
    for g in range(NSA_GROUPS):
        ks, vs = slice(g * hd, (g + 1) * hd), slice(NSA_KV_W + g * hd, NSA_KV_W + (g + 1) * hd)
        qg = qs[g]
        c_sel = _online_step(carry[g], _dot(qg, seln_ref[0, ks, :].astype(bf16)), heads(causal_new),
                             seln_ref[0, vs, :].astype(bf16))
        o_sel = c_sel[2] / jnp.maximum(c_sel[1], 1e-30)
        c_win = _online_step(_online_init(rows, hd), _dot(qg, wst_ref[0, 0, ks, :].astype(bf16)),
                             heads(win_mask), wst_ref[0, 0, vs, :].astype(bf16))
        c_win = _online_step(c_win, _dot(qg, winn_ref[0, ks, :].astype(bf16)), heads(causal_new),
                             winn_ref[0, vs, :].astype(bf16))
        o_win = c_win[2] / jnp.maximum(c_win[1], 1e-30)
        for r in range(NSA_REP):
            hh = g * NSA_REP + r
            sl = slice(r * t_new, (r + 1) * t_new)
            o = (gates[:, hh:hh + 1] * o_cmps[g][sl]
                 + gates[:, NSA_HEADS + hh:NSA_HEADS + hh + 1] * o_sel[sl]
                 + gates[:, 2 * NSA_HEADS + hh:2 * NSA_HEADS + hh + 1] * o_win[sl])
            o_ref[:, hh * hd:(hh + 1) * hd] = o.astype(o_ref.dtype)


def _nsa_sample(page_table, qa, gates, kcm, cmp_new, sel_new, win_new, win_state_t, cache_t, layer, t_new):
    n_dec, n_pages = page_table.shape
    n_chunks = n_pages // PAGES_PER_CHUNK
    past_len = n_pages * PAGE_SIZE
    nbp = past_len // NSA_BLOCK
    wlen = win_state_t.shape[-1]
    ck = PAGES_PER_CHUNK * PAGE_SIZE
    new_spec = pl.BlockSpec((1, NSA_ROW_W, t_new), lambda b, pt: (b, 0, 0))
    return pl.pallas_call(
        functools.partial(_nsa_sample_body, layer=layer, n_chunks=n_chunks, past_len=past_len, t_new=t_new),
        out_shape=jax.ShapeDtypeStruct((n_dec * t_new, NSA_HEADS * NSA_HEAD_DIM), f32),
        grid_spec=pltpu.PrefetchScalarGridSpec(
            num_scalar_prefetch=1,
            grid=(n_dec,),
            in_specs=[
                pl.BlockSpec((NSA_HEADS, t_new, NSA_HEAD_DIM), lambda b, pt: (0, b, 0)),
                pl.BlockSpec((t_new, LANES), lambda b, pt: (b, 0)),
                pl.BlockSpec((1, NSA_ROW_W, nbp), lambda b, pt: (b, 0, 0)),
                new_spec, new_spec, new_spec,
                pl.BlockSpec((1, 1, NSA_ROW_W, wlen), lambda b, pt: (layer, b, 0, 0)),
                pl.BlockSpec(memory_space=pl.ANY),
            ],
            out_specs=pl.BlockSpec((t_new, NSA_HEADS * NSA_HEAD_DIM), lambda b, pt: (b, 0)),
            scratch_shapes=[pltpu.VMEM((2, NSA_ROW_W, ck), f32), pltpu.SemaphoreType.DMA((2,))],
        ),
        compiler_params=_cparams("arbitrary"),
        name="nsa_sample",
    )(page_table, qa, gates, kcm, cmp_new, sel_new, win_new, win_state_t, cache_t)


def _mla_sample_body(pt_ref, qm_ref, latn_ref, cache_ref, o_ref, buf, sem, *, layer, n_chunks, t_new):
    b = pl.program_id(0)
    rows = MLA_HEADS * t_new
    q = qm_ref[...].reshape(rows, MLA_LAT_W).astype(bf16)

    def fn(chunk, slab, carry):
        k_t = slab.astype(bf16)
        return _online_step(carry, _dot(q, k_t), True, k_t[:MLA_KV_RANK])

    carry = _stream_chunks(pt_ref, cache_ref, layer, b, n_chunks, buf, sem, fn, _online_init(rows, MLA_KV_RANK))
    t_row = lax.broadcasted_iota(jnp.int32, (t_new, 1), 0)
    t_col = lax.broadcasted_iota(jnp.int32, (1, t_new), 1)
    causal = jnp.broadcast_to((t_col <= t_row)[None], (MLA_HEADS, t_new, t_new)).reshape(rows, t_new)
    k_new = latn_ref[0].astype(bf16)
    _, l_i, acc = _online_step(carry, _dot(q, k_new), causal, k_new[:MLA_KV_RANK])
    o = acc / jnp.maximum(l_i, 1e-30)
    for hh in range(MLA_HEADS):
        o_ref[:, hh * MLA_KV_RANK:(hh + 1) * MLA_KV_RANK] = o[hh * t_new:(hh + 1) * t_new].astype(o_ref.dtype)


def _mla_sample(page_table, qm, lat_new, cache_t, layer, t_new):
    n_dec, n_pages = page_table.shape
    n_chunks = n_pages // PAGES_PER_CHUNK
    ck = PAGES_PER_CHUNK * PAGE_SIZE
    return pl.pallas_call(
        functools.partial(_mla_sample_body, layer=layer, n_chunks=n_chunks, t_new=t_new),
        out_shape=jax.ShapeDtypeStruct((n_dec * t_new, MLA_HEADS * MLA_KV_RANK), f32),
        grid_spec=pltpu.PrefetchScalarGridSpec(
            num_scalar_prefetch=1,
            grid=(n_dec,),
            in_specs=[
                pl.BlockSpec((MLA_HEADS, t_new, MLA_LAT_W), lambda b, pt: (0, b, 0)),
                pl.BlockSpec((1, MLA_LAT_W, t_new), lambda b, pt: (b, 0, 0)),
                pl.BlockSpec(memory_space=pl.ANY),
            ],
            out_specs=pl.BlockSpec((t_new, MLA_HEADS * MLA_KV_RANK), lambda b, pt: (b, 0)),
            scratch_shapes=[pltpu.VMEM((2, MLA_LAT_W, ck), f32), pltpu.SemaphoreType.DMA((2,))],
        ),
        compiler_params=_cparams("arbitrary"),
        name="mla_sample",
    )(page_table, qm, lat_new, cache_t)


def _rope_tables(pos):
    pos = pos.astype(f32)

    def cs(dim):
        inv = ROPE_THETA ** (-jnp.arange(0, dim, 2, dtype=f32) / dim)
        ang = pos[:, None] * inv[None, :]
        return jnp.cos(ang), jnp.sin(ang)

    ca, sa = cs(NSA_HEAD_DIM)
    cb, sb = cs(MLA_ROPE_DIM)
    rep_a = LANES // NSA_HEAD_DIM
    rep_b = LANES // MLA_ROPE_DIM
    return {
        "ca": jnp.tile(ca, (1, 2 * rep_a)), "sa": jnp.tile(jnp.concatenate([-sa, sa], axis=1), (1, rep_a)),
        "cb": jnp.tile(cb, (1, 2 * rep_b)), "sb": jnp.tile(jnp.concatenate([-sb, sb], axis=1), (1, rep_b)),
        "cat": ca.T, "sat": sa.T, "cbt": cb.T, "sbt": sb.T,
    }


def _layer_weights(l, p):
    d = p["w_in"].shape[1]
    w_in = p["w_in"][l]
    nq = NSA_HEADS * NSA_HEAD_DIM
    sizes = (nq, 6 * NSA_KV_W, 3 * NSA_HEADS, MLA_Q_RANK, MLA_KV_RANK, MLA_ROPE_DIM, 2 * d)
    o = np.cumsum((0,) + sizes)
    w_q, w_kv, w_gate, w_cq, w_ckv, w_kr, w_merge = (w_in[:, o[i]:o[i + 1]] for i in range(7))
    w_gate = w_gate.reshape(d, NSA_HEADS, 3).transpose(0, 2, 1).reshape(d, 3 * NSA_HEADS)
    w_gate = jnp.pad(w_gate, ((0, 0), (0, LANES - 3 * NSA_HEADS)))
    ws = jnp.concatenate([w_q, w_cq, w_gate], axis=1).astype(bf16)
    wt = jnp.concatenate([w_kv, w_ckv, w_kr], axis=1).T.astype(bf16)
    w_uq = p["w_uq"][l].reshape(MLA_Q_RANK, MLA_HEADS, MLA_NOPE_DIM + MLA_ROPE_DIM)
    wuq = jnp.concatenate([w_uq[:, :, :MLA_NOPE_DIM].reshape(MLA_Q_RANK, -1),
                           w_uq[:, :, MLA_NOPE_DIM:].reshape(MLA_Q_RANK, -1)], axis=1).astype(bf16)
    eye = jnp.eye(MLA_HEADS, dtype=f32)
    wuk_bd = jnp.einsum("chn,hk->hnkc", p["w_uk"][l], eye).reshape(
        MLA_HEADS * MLA_NOPE_DIM, MLA_HEADS * MLA_KV_RANK).astype(bf16)
    wuv_bd = jnp.einsum("chv,hk->hckv", p["w_uv"][l], eye).reshape(
        MLA_HEADS * MLA_KV_RANK, MLA_HEADS * MLA_V_DIM).astype(bf16)
    return {
        "g_ffn_a": p["g_ffn_a"][l][None], "w13_a": p["w13_a"][l].astype(bf16), "w2_a": p["w2_a"][l].astype(bf16),
        "g_ffn_b": p["g_ffn_b"][l][None], "w13_b": p["w13_b"][l].astype(bf16), "w2_b": p["w2_b"][l].astype(bf16),
        "g_mix": p["g_mix"][l][None], "ws": ws, "wt": wt, "wm": w_merge.astype(bf16),
        "g_cq": p["g_cq"][l][None], "wuq": wuq, "wuk_bd": wuk_bd, "g_ckv": p["g_ckv"][l][:, None],
        "wuv_bd": wuv_bd, "w_o_nsa": p["w_o_nsa"][l].astype(bf16), "w_o_mla": p["w_o_mla"][l].astype(bf16),
        "w_out": p["w_out"][l].astype(bf16),
    }


def _tile(n, want):
    t = min(n, want)
    assert n % t == 0, (n, want)
    return t


def _prompt_trunk(x_prompt, p):
    bsz, seq_len, d = x_prompt.shape
    assert seq_len % LANES == 0 and seq_len % NSA_BLOCK == 0
    depth = p["w_in"].shape[0]
    n = bsz * seq_len
    x = x_prompt.reshape(n, d)
    tabs = _rope_tables(jnp.arange(seq_len, dtype=jnp.int32))
    tm_ffn = _tile(n, 1024)
    tf = _tile(p["w2_a"].shape[1], 256)
    tm = _tile(seq_len, 512)
    tq = LANES
    kc = _tile(seq_len, 512)
    cmp_rows, sel_rows, lat_rows, win_rows = [], [], [], []
    for l in range(depth):
        wl = _layer_weights(l, p)
        x = _ffn(x, wl["g_ffn_a"], wl["w13_a"], wl["w2_a"], wl["g_ffn_a"], final_norm=False, tm=tm_ffn, tf=tf)
        qa, gates, qm, cmp_t, sel_t, win_t, lat_t = _proj(x, seq_len, wl, tabs, tm=tm, q_dtype=bf16)
        o_nsa = _nsa_prompt(qa, gates, cmp_t, sel_t, win_t, tq=tq, kc=kc)
        o_lat = _mla_prompt(qm, lat_t, tq=tq, kc=kc)
        x = _merge(x, o_nsa, o_lat, wl, tm=tm)
        last = l == depth - 1
        x = _ffn(x, wl["g_ffn_b"], wl["w13_b"], wl["w2_b"], p["g_final"][None], final_norm=last, tm=tm_ffn, tf=tf)
        cmp_rows.append(cmp_t)
        sel_rows.append(sel_t)
        lat_rows.append(lat_t)
        win_rows.append(win_t)
    return x.reshape(bsz, seq_len, d), cmp_rows, sel_rows, lat_rows, win_rows


def _per_seq(rows_t, n_dec, t_new):
    f = rows_t.shape[1]
    return jnp.transpose(rows_t[0].reshape(f, n_dec, t_new), (1, 0, 2))


def _sample_trunk(x_sample, cmp_c, sel_c, mla_c, win_state_t, page_table, p):
    n_dec, t_new, d = x_sample.shape
    n_pages = page_table.shape[1]
    assert t_new < NSA_BLOCK and n_pages % PAGES_PER_CHUNK == 0 and PAGE_SIZE % NSA_BLOCK == 0
    depth = p["w_in"].shape[0]
    n = n_dec * t_new
    past_len = n_pages * PAGE_SIZE
    x = x_sample.reshape(n, d)
    pos = past_len + jnp.arange(t_new, dtype=jnp.int32)
    tabs = _rope_tables(jnp.tile(pos, n_dec))
    tm = _tile(n, 512)
    tm_ffn = _tile(n, 1024)
    tf = _tile(p["w2_a"].shape[1], 256)
    cmp_rows, sel_rows, lat_rows, win_rows = [], [], [], []
    for l in range(depth):
        wl = _layer_weights(l, p)
        x = _ffn(x, wl["g_ffn_a"], wl["w13_a"], wl["w2_a"], wl["g_ffn_a"], final_norm=False, tm=tm_ffn, tf=tf)
        qa, gates, qm, cmp_t, sel_t, win_t, lat_t = _proj(x, n, wl, tabs, tm=tm, q_dtype=f32)
        cmp_n, sel_n, win_n, lat_n = (_per_seq(a, n_dec, t_new) for a in (cmp_t, sel_t, win_t, lat_t))
        kcm = _cmp_means(page_table, cmp_c, l)
        o_nsa = _nsa_sample(page_table, qa, gates, kcm, cmp_n, sel_n, win_n, win_state_t, sel_c, l, t_new)
        o_lat = _mla_sample(page_table, qm, lat_n, mla_c, l, t_new)
        x = _merge(x, o_nsa, o_lat, wl, tm=tm)
        last = l == depth - 1
        x = _ffn(x, wl["g_ffn_b"], wl["w13_b"], wl["w2_b"], p["g_final"][None], final_norm=last, tm=tm_ffn, tf=tf)
        cmp_rows.append(cmp_n)
        sel_rows.append(sel_n)
        lat_rows.append(lat_n)
        win_rows.append(jnp.concatenate([win_state_t[l][:, :, t_new:], win_n], axis=-1))
    return x.reshape(n_dec, t_new, d), cmp_rows, sel_rows, lat_rows, win_rows


def _kv_rows_out(rows_t):
    t = jnp.stack(rows_t, axis=1)
    b, depth, _, s = t.shape
    t = t.reshape(b, depth, 2, NSA_GROUPS, NSA_HEAD_DIM, s)
    return jnp.transpose(t, (0, 5, 1, 2, 3, 4))


def _lat_rows_out(rows_t):
    return jnp.transpose(jnp.stack(rows_t, axis=1), (0, 3, 1, 2))


def _win_out(rows_t):
    t = jnp.stack(rows_t, axis=0)
    depth, b, _, w = t.shape
    t = t.reshape(depth, b, 2, NSA_GROUPS, NSA_HEAD_DIM, w)
    return jnp.transpose(t, (0, 1, 5, 2, 3, 4))


def _last_rows(rows_t, wlen):
    s = rows_t.shape[-1]
    if s >= wlen:
        return rows_t[:, :, s - wlen:]
    return jnp.pad(rows_t, ((0, 0), (0, 0), (wlen - s, 0)))


def kernel(x_prompt, x_sample, cache_nsa_cmp, cache_nsa_sel, cache_mla, state_nsa_win, page_table,
           g_ffn_a, w13_a, w2_a, g_mix, w_in, g_cq, w_uq, g_ckv, w_uk, w_uv,
           w_o_nsa, w_o_mla, w_out, g_ffn_b, w13_b, w2_b, g_final):
    p = dict(g_ffn_a=g_ffn_a, w13_a=w13_a, w2_a=w2_a, g_mix=g_mix, w_in=w_in, g_cq=g_cq, w_uq=w_uq,
             g_ckv=g_ckv, w_uk=w_uk, w_uv=w_uv, w_o_nsa=w_o_nsa, w_o_mla=w_o_mla, w_out=w_out,
             g_ffn_b=g_ffn_b, w13_b=w13_b, w2_b=w2_b, g_final=g_final)
    assert cache_nsa_cmp.shape[1] == PAGE_SIZE
    n_pool, _, depth = cache_nsa_cmp.shape[:3]
    wlen = state_nsa_win.shape[2]
    cmp_c = jnp.transpose(cache_nsa_cmp, (0, 2, 3, 4, 5, 1)).reshape(n_pool, depth, NSA_ROW_W, PAGE_SIZE)
    sel_c = jnp.transpose(cache_nsa_sel, (0, 2, 3, 4, 5, 1)).reshape(n_pool, depth, NSA_ROW_W, PAGE_SIZE)
    mla_c = jnp.transpose(cache_mla, (0, 2, 3, 1))
    win_state_t = jnp.transpose(state_nsa_win, (0, 1, 3, 4, 5, 2)).reshape(
        depth, state_nsa_win.shape[1], NSA_ROW_W, wlen)

    y_p, cmp_p, sel_p, lat_p, win_p = _prompt_trunk(x_prompt, p)
    y_s, cmp_s, sel_s, lat_s, win_s = _sample_trunk(x_sample, cmp_c, sel_c, mla_c, win_state_t, page_table, p)
    win_p = [_last_rows(w, wlen) for w in win_p]
    return (y_p, y_s,
            _kv_rows_out(cmp_p), _kv_rows_out(cmp_s),
            _kv_rows_out(sel_p), _kv_rows_out(sel_s),
            _lat_rows_out(lat_p), _lat_rows_out(lat_s),
            _win_out(win_p), _win_out(win_s))
```

```python
import functools
import math

import jax
import jax.numpy as jnp
import numpy as np
from jax import lax
from jax.experimental import pallas as pl
from jax.experimental.pallas import tpu as pltpu

f32 = jnp.float32
bf16 = jnp.bfloat16

NSA_HEADS = 8
NSA_GROUPS = 2
NSA_HEAD_DIM = 64
NSA_REP = NSA_HEADS // NSA_GROUPS
NSA_BLOCK = 64
NSA_TOP_K = 16
NSA_WINDOW = 512
MLA_HEADS = 8
MLA_Q_RANK = 256
MLA_KV_RANK = 128
MLA_NOPE_DIM = 64
MLA_ROPE_DIM = 32
MLA_V_DIM = 64
MLA_LAT_W = MLA_KV_RANK + MLA_ROPE_DIM
ROPE_THETA = 10000.0
RMS_EPS = 1e-6
PAGE_SIZE = 128
NSA_KV_W = NSA_GROUPS * NSA_HEAD_DIM
NSA_ROW_W = 2 * NSA_KV_W
N_GATES = 3 * NSA_HEADS

LANES = 128
VMEM_LIMIT_BYTES = 56 * 1024 * 1024

NEG_BIG = -1e30
LOG2E = math.log2(math.e)


def _cparams(*sem):
    return pltpu.CompilerParams(dimension_semantics=sem, vmem_limit_bytes=VMEM_LIMIT_BYTES)


def _rms(x, g):
    ms = jnp.mean(x * x, axis=-1, keepdims=True)
    return x * lax.rsqrt(ms + RMS_EPS) * g


def _dot(a, b):
    return jnp.dot(a, b, preferred_element_type=f32)


def _dot_nt(a, b):
    return lax.dot_general(a, b, (((1,), (1,)), ((), ())), preferred_element_type=f32)


def _ffn_body(x_ref, g_ref, w1_ref, w3_ref, w2_ref, gf_ref, o_ref, *, final_norm, n_split):
    x = x_ref[...]
    h = _rms(x, g_ref[...]).astype(bf16)
    dff = w2_ref.shape[0]
    tf = dff // n_split
    acc = None
    for j in range(n_split):
        a = _dot(h, w1_ref[:, j * tf:(j + 1) * tf])
        b = _dot(h, w3_ref[:, j * tf:(j + 1) * tf])
        u = (a * jax.nn.sigmoid(a) * b).astype(bf16)
        part = _dot(u, w2_ref[j * tf:(j + 1) * tf, :])
        acc = part if acc is None else acc + part
    y = x + 0.5 * acc
    if final_norm:
        y = _rms(y, gf_ref[...])
    o_ref[...] = y


def _ffn(x, g, w13, w2, gf, *, final_norm, tm, n_split):
    n, d = x.shape
    dff = w2.shape[0]
    resident = pl.Buffered(1)
    return pl.pallas_call(
        functools.partial(_ffn_body, final_norm=final_norm, n_split=n_split),
        out_shape=jax.ShapeDtypeStruct((n, d), f32),
        grid=(n // tm,),
        in_specs=[
            pl.BlockSpec((tm, d), lambda i: (i, 0)),
            pl.BlockSpec((1, d), lambda i: (0, 0)),
            pl.BlockSpec((d, dff), lambda i: (0, 0), pipeline_mode=resident),
            pl.BlockSpec((d, dff), lambda i: (0, 1), pipeline_mode=resident),
            pl.BlockSpec((dff, d), lambda i: (0, 0), pipeline_mode=resident),
            pl.BlockSpec((1, d), lambda i: (0, 0)),
        ],
        out_specs=pl.BlockSpec((tm, d), lambda i: (i, 0)),
        compiler_params=_cparams("arbitrary"),
        name="ffn",
    )(x, g, w13, w13, w2, gf)


def _rope_lanes(x, cos, sin_signed, half):
    w = x.shape[-1]
    lane = lax.broadcasted_iota(jnp.int32, x.shape, x.ndim - 1)
    first = (lane % (2 * half)) < half
    rot = jnp.where(first, pltpu.roll(x, w - half, x.ndim - 1), pltpu.roll(x, half, x.ndim - 1))
    return x * cos + rot * sin_signed


def _rope_rows(x, cos, sin, half):
    x1, x2 = x[:half], x[half:]
    return jnp.concatenate([x1 * cos - x2 * sin, x2 * cos + x1 * sin], axis=0)


def _proj_body(x_ref, g_ref, ws_ref, wt_ref, gcq_ref, wuq_ref, wuk_ref, gckv_ref,
               ca_ref, sa_ref, cb_ref, sb_ref, cat_ref, sat_ref, cbt_ref, sbt_ref,
               qa_ref, gate_ref, qm_ref, cmp_ref, sel_ref, win_ref, lat_ref):
    h = _rms(x_ref[...], g_ref[...]).astype(bf16)
    ps = _dot(h, ws_ref[...])
    pt = _dot_nt(wt_ref[...], h)
    nq = NSA_HEADS * NSA_HEAD_DIM
    scale_a = NSA_HEAD_DIM ** -0.5 * LOG2E
    ca, sa = ca_ref[...], sa_ref[...]
    for c in range(nq // LANES):
        q = _rope_lanes(ps[:, c * LANES:(c + 1) * LANES], ca, sa, NSA_HEAD_DIM // 2) * scale_a
        per = LANES // NSA_HEAD_DIM
        for r in range(per):
            qa_ref[c * per + r] = q[:, r * NSA_HEAD_DIM:(r + 1) * NSA_HEAD_DIM].astype(qa_ref.dtype)
    gate_ref[...] = jax.nn.sigmoid(ps[:, nq + MLA_Q_RANK:nq + MLA_Q_RANK + LANES])
    cqn = _rms(ps[:, nq:nq + MLA_Q_RANK], gcq_ref[...]).astype(bf16)
    qb = _dot(cqn, wuq_ref[...])
    n_nope = MLA_HEADS * MLA_NOPE_DIM
    scale_b = (MLA_NOPE_DIM + MLA_ROPE_DIM) ** -0.5 * LOG2E
    q_lat = _dot(qb[:, :n_nope].astype(bf16), wuk_ref[...]) * scale_b
    cb, sb = cb_ref[...], sb_ref[...]
    per = LANES // MLA_ROPE_DIM
    for c in range(MLA_HEADS // per):
        lo = n_nope + c * LANES
        q_rope = _rope_lanes(qb[:, lo:lo + LANES], cb, sb, MLA_ROPE_DIM // 2) * scale_b
        for r in range(per):
            qm_ref[c * per + r, :, MLA_KV_RANK:MLA_LAT_W] = (
                q_rope[:, r * MLA_ROPE_DIM:(r + 1) * MLA_ROPE_DIM].astype(qm_ref.dtype))
    for hh in range(MLA_HEADS):
        qm_ref[hh, :, 0:MLA_KV_RANK] = q_lat[:, hh * MLA_KV_RANK:(hh + 1) * MLA_KV_RANK].astype(qm_ref.dtype)
    cat, sat = cat_ref[...], sat_ref[...]
    for br, o_ref in enumerate((cmp_ref, sel_ref, win_ref)):
        base = br * NSA_ROW_W
        for gi in range(NSA_GROUPS):
            lo = base + gi * NSA_HEAD_DIM
            o_ref[0, gi * NSA_HEAD_DIM:(gi + 1) * NSA_HEAD_DIM, :] = _rope_rows(
                pt[lo:lo + NSA_HEAD_DIM], cat, sat, NSA_HEAD_DIM // 2)
        o_ref[0, NSA_KV_W:NSA_ROW_W, :] = pt[base + NSA_KV_W:base + NSA_ROW_W]
    base = 3 * NSA_ROW_W
    ckv = pt[base:base + MLA_KV_RANK]
    ms = jnp.mean(ckv * ckv, axis=0, keepdims=True)
    lat_ref[0, 0:MLA_KV_RANK, :] = ckv * lax.rsqrt(ms + RMS_EPS) * gckv_ref[...]
    lat_ref[0, MLA_KV_RANK:MLA_LAT_W, :] = _rope_rows(
        pt[base + MLA_KV_RANK:base + MLA_LAT_W], cbt_ref[...], sbt_ref[...], MLA_ROPE_DIM // 2)


def _proj(x, seq_len, wl, tabs, *, tm, q_dtype):
    n, d = x.shape
    nper = seq_len // tm
    n_seq = n // seq_len
    tok = lambda i: (i, 0)
    per = lambda i: (i % nper, 0)
    per_t = lambda i: (0, i % nper)
    const = lambda i: (0, 0)
    kv_map = lambda i: (i // nper, 0, i % nper)
    ws, wt, wuq, wuk = wl["ws"], wl["wt"], wl["wuq"], wl["wuk_bd"]
    out_shape = (
        jax.ShapeDtypeStruct((NSA_HEADS, n, NSA_HEAD_DIM), q_dtype),
        jax.ShapeDtypeStruct((n, LANES), f32),
        jax.ShapeDtypeStruct((MLA_HEADS, n, MLA_LAT_W), q_dtype),
        jax.ShapeDtypeStruct((n_seq, NSA_ROW_W, seq_len), f32),
        jax.ShapeDtypeStruct((n_seq, NSA_ROW_W, seq_len), f32),
        jax.ShapeDtypeStruct((n_seq, NSA_ROW_W, seq_len), f32),
        jax.ShapeDtypeStruct((n_seq, MLA_LAT_W, seq_len), f32),
    )
    out_specs = (
        pl.BlockSpec((NSA_HEADS, tm, NSA_HEAD_DIM), lambda i: (0, i, 0)),
        pl.BlockSpec((tm, LANES), tok),
        pl.BlockSpec((MLA_HEADS, tm, MLA_LAT_W), lambda i: (0, i, 0)),
        pl.BlockSpec((1, NSA_ROW_W, tm), kv_map),
        pl.BlockSpec((1, NSA_ROW_W, tm), kv_map),
        pl.BlockSpec((1, NSA_ROW_W, tm), kv_map),
        pl.BlockSpec((1, MLA_LAT_W, tm), kv_map),
    )
    in_specs = [
        pl.BlockSpec((tm, d), tok),
        pl.BlockSpec((1, d), const),
        pl.BlockSpec(ws.shape, const),
        pl.BlockSpec(wt.shape, const),
        pl.BlockSpec((1, MLA_Q_RANK), const),
        pl.BlockSpec(wuq.shape, const),
        pl.BlockSpec(wuk.shape, const),
        pl.BlockSpec((MLA_KV_RANK, 1), const),
        pl.BlockSpec((tm, LANES), per), pl.BlockSpec((tm, LANES), per),
        pl.BlockSpec((tm, LANES), per), pl.BlockSpec((tm, LANES), per),
        pl.BlockSpec((NSA_HEAD_DIM // 2, tm), per_t), pl.BlockSpec((NSA_HEAD_DIM // 2, tm), per_t),
        pl.BlockSpec((MLA_ROPE_DIM // 2, tm), per_t), pl.BlockSpec((MLA_ROPE_DIM // 2, tm), per_t),
    ]
    return pl.pallas_call(
        _proj_body,
        out_shape=out_shape,
        grid=(n // tm,),
        in_specs=in_specs,
        out_specs=out_specs,
        compiler_params=_cparams("arbitrary"),
        name="proj",
    )(x, wl["g_mix"], ws, wt, wl["g_cq"], wuq, wuk, wl["g_ckv"],
      tabs["ca"], tabs["sa"], tabs["cb"], tabs["sb"], tabs["cat"], tabs["sat"], tabs["cbt"], tabs["sbt"])


def _merge_body(x_ref, g_ref, wm_ref, on_ref, ol_ref, won_ref, wuv_ref, wom_ref, wout_ref, o_ref):
    x = x_ref[...]
    d = x.shape[-1]
    h = _rms(x, g_ref[...]).astype(bf16)
    mg = _dot(h, wm_ref[...])
    ga = jax.nn.sigmoid(mg[:, :d])
    gb = jax.nn.sigmoid(mg[:, d:])
    a = _dot(on_ref[...].astype(bf16), won_ref[...])
    om = _dot(ol_ref[...].astype(bf16), wuv_ref[...]).astype(bf16)
    b = _dot(om, wom_ref[...])
    m = (ga * a + gb * b).astype(bf16)
    o_ref[...] = x + _dot(m, wout_ref[...])


def _merge(x, o_nsa, o_lat, wl, *, tm):
    n, d = x.shape
    tok = lambda i: (i, 0)
    const = lambda i: (0, 0)
    ws = [wl["wm"], wl["w_o_nsa"], wl["wuv_bd"], wl["w_o_mla"], wl["w_out"]]
    return pl.pallas_call(
        _merge_body,
        out_shape=jax.ShapeDtypeStruct((n, d), f32),
        grid=(n // tm,),
        in_specs=[
            pl.BlockSpec((tm, d), tok),
            pl.BlockSpec((1, d), const),
            pl.BlockSpec(ws[0].shape, const),
            pl.BlockSpec((tm, o_nsa.shape[1]), tok),
            pl.BlockSpec((tm, o_lat.shape[1]), tok),
            pl.BlockSpec(ws[1].shape, const),
            pl.BlockSpec(ws[2].shape, const),
            pl.BlockSpec(ws[3].shape, const),
            pl.BlockSpec(ws[4].shape, const),
        ],
        out_specs=pl.BlockSpec((tm, d), tok),
        compiler_params=_cparams("arbitrary"),
        name="merge",
    )(x, wl["g_mix"], ws[0], o_nsa, o_lat, ws[1], ws[2], ws[3], ws[4])


def _online_step(carry, s, v_t):
    m_i, l_i, acc = carry
    m_new = jnp.maximum(m_i, jnp.max(s, axis=-1, keepdims=True))
    p = jnp.exp2(s - m_new)
    alpha = jnp.exp2(m_i - m_new)
    l_new = alpha * l_i + jnp.sum(p, axis=-1, keepdims=True)
    acc_new = alpha * acc + _dot_nt(p.astype(bf16), v_t)
    return m_new, l_new, acc_new


def _online_init(rows, dv):
    return (jnp.full((rows, 1), NEG_BIG, f32), jnp.zeros((rows, 1), f32), jnp.zeros((rows, dv), f32))


def _bias_heads(s, bias, n_heads):
    t, k = bias.shape
    return (s.reshape(n_heads, t, k) + bias[None]).reshape(n_heads * t, k)


def _hi_lo(x):
    hi = x.astype(bf16)
    return hi, (x - hi.astype(f32)).astype(bf16)


def _block_avg_matrix(n_keys, n_cols):
    row = lax.broadcasted_iota(jnp.int32, (n_keys, n_cols), 0)
    col = lax.broadcasted_iota(jnp.int32, (n_keys, n_cols), 1)
    return jnp.where(row // NSA_BLOCK == col, 1.0 / NSA_BLOCK, 0.0).astype(bf16)


def _block_expand_matrix(n_rows, n_keys, key0):
    eb = lax.broadcasted_iota(jnp.int32, (n_rows, n_keys), 0)
    ek = key0 + lax.broadcasted_iota(jnp.int32, (n_rows, n_keys), 1)
    return jnp.where(ek // NSA_BLOCK == eb, 1.0, 0.0).astype(bf16)


def _nsa_prompt_body(qa_ref, gate_ref, cmp_ref, sel_ref, win_ref, o_ref, kcm_sc, sel_sc, win_sc,
                     *, tq, kc, seq_len):
    c = pl.program_id(1)
    nb = seq_len // NSA_BLOCK
    hd = NSA_HEAD_DIM
    rows = NSA_REP * tq

    @pl.when(c == 0)
    def _():
        avg = _block_avg_matrix(seq_len, LANES)
        hi, lo = _hi_lo(cmp_ref[0])
        kcm_sc[...] = (_dot(hi, avg) + _dot(lo, avg)).astype(bf16)
        for ch in range(seq_len // kc):
            sel_sc[ch] = sel_ref[0, :, ch * kc:(ch + 1) * kc].astype(bf16)
        for ch in range(seq_len // LANES):
            win_sc[ch] = win_ref[0, :, ch * LANES:(ch + 1) * LANES].astype(bf16)

    t0 = c * tq
    tpos = t0 + lax.broadcasted_iota(jnp.int32, (tq, 1), 0)
    blk = lax.broadcasted_iota(jnp.int32, (1, LANES), 1)
    complete = ((blk + 1) * NSA_BLOCK - 1 <= tpos) & (blk < nb)
    nbr = -(-nb // 8) * 8
    blk_t = lax.broadcasted_iota(jnp.int32, (nbr, tq), 0)
    tpos_t = t0 + lax.broadcasted_iota(jnp.int32, (1, tq), 1)
    complete_t = ((blk_t + 1) * NSA_BLOCK - 1 <= tpos_t) & (blk_t < nb)
    forced_t = ((blk_t == tpos_t // NSA_BLOCK) | (blk_t == 0)) & (blk_t < nb)
    gates = gate_ref[...]
    n_sel_chunks = (t0 + tq + kc - 1) // kc
    n_band = NSA_WINDOW // LANES + tq // LANES
    band0 = c - (n_band - 1)
    kpos_w = band0 * LANES + lax.broadcasted_iota(jnp.int32, (1, n_band * LANES), 1)
    bias_w = jnp.where((kpos_w >= 0) & (kpos_w <= tpos) & (kpos_w > tpos - NSA_WINDOW), 0.0, NEG_BIG)

    for g in range(NSA_GROUPS):
        ks, vs = slice(g * hd, (g + 1) * hd), slice(NSA_KV_W + g * hd, NSA_KV_W + (g + 1) * hd)
        qg = qa_ref[g * NSA_REP:(g + 1) * NSA_REP].reshape(rows, hd)
        s = _dot(qg, kcm_sc[ks, :])
        vc_t = kcm_sc[vs, :]
        imp = jnp.zeros((tq, LANES), f32)
        o_cmp = []
        for r in range(NSA_REP):
            sr = s[r * tq:(r + 1) * tq]
            m = jnp.max(jnp.where(complete, sr, -jnp.inf), axis=-1, keepdims=True)
            m = jnp.where(m == -jnp.inf, 0.0, m)
            e = jnp.where(complete, jnp.exp2(sr - m), 0.0)
            p = e / jnp.maximum(jnp.sum(e, axis=-1, keepdims=True), 1e-30)
            imp = imp + p
            o_cmp.append(_dot_nt(p.astype(bf16), vc_t))
        v = jnp.where(forced_t, jnp.inf, jnp.where(complete_t, imp.T[:nbr], -jnp.inf))
        rank = jnp.zeros((nbr, tq), jnp.int32)
        for i in range(nb):
            row = v[i:i + 1, :]
            beats = (row > v) | ((row == v) & (blk_t > i))
            rank = rank + beats.astype(jnp.int32)
        chosen_t = jnp.where((rank < min(NSA_TOP_K, nb)) & (blk_t < nb), 1.0, 0.0)
        chosen = jnp.concatenate([chosen_t, jnp.zeros((LANES - nbr, tq), f32)], axis=0).T.astype(bf16)

        def sel_step(ch, carry):
            k_t = sel_sc[ch, ks, :]
            v_t = sel_sc[ch, vs, :]
            kpos = ch * kc + lax.broadcasted_iota(jnp.int32, (1, kc), 1)
            hit = _dot(chosen, _block_expand_matrix(LANES, kc, ch * kc))
            bias = jnp.where((hit > 0.5) & (kpos <= tpos), 0.0, NEG_BIG)
            return _online_step(carry, _bias_heads(_dot(qg, k_t), bias, NSA_REP), v_t)

        carry = lax.fori_loop(0, n_sel_chunks // 2, lambda i, cr: sel_step(2 * i + 1, sel_step(2 * i, cr)),
                              _online_init(rows, hd))
        _, l_s, acc_s = lax.cond(n_sel_chunks % 2 == 1, lambda cr: sel_step(n_sel_chunks - 1, cr),
                                 lambda cr: cr, carry)

        idc = [jnp.maximum(band0 + j, 0) for j in range(n_band)]
        k_band = jnp.concatenate([win_sc[i, ks, :] for i in idc], axis=1)
        v_band = jnp.concatenate([win_sc[i, vs, :] for i in idc], axis=1)
        sw = _bias_heads(_dot(qg, k_band), bias_w, NSA_REP)
        pw = jnp.exp2(sw - jnp.max(sw, axis=-1, keepdims=True))
        l_w = jnp.sum(pw, axis=-1, keepdims=True)
        acc_w = _dot_nt(pw.astype(bf16), v_band)

        r_s = 1.0 / l_s
        r_w = 1.0 / l_w
        for r in range(NSA_REP):
            hh = g * NSA_REP + r
            sl = slice(r * tq, (r + 1) * tq)

            def gate(branch):
                k = branch * NSA_HEADS + hh
                return gates[:, k:k + 1]

            o = gate(0) * o_cmp[r] + gate(1) * (acc_s[sl] * r_s[sl]) + gate(2) * (acc_w[sl] * r_w[sl])
            o_ref[:, hh * hd:(hh + 1) * hd] = o.astype(o_ref.dtype)


def _nsa_prompt(qa, gates, cmp_t, sel_t, win_t, *, tq, kc):
    n_seq, _, seq_len = cmp_t.shape
    n = n_seq * seq_len
    nc = seq_len // tq
    assert seq_len // NSA_BLOCK <= LANES and tq % LANES == 0
    kv_spec = pl.BlockSpec((1, NSA_ROW_W, seq_len), lambda b, c: (b, 0, 0))
    return pl.pallas_call(
        functools.partial(_nsa_prompt_body, tq=tq, kc=kc, seq_len=seq_len),
        out_shape=jax.ShapeDtypeStruct((n, NSA_HEADS * NSA_HEAD_DIM), bf16),
        grid=(n_seq, nc),
        in_specs=[
            pl.BlockSpec((NSA_HEADS, tq, NSA_HEAD_DIM), lambda b, c: (0, b * nc + c, 0)),
            pl.BlockSpec((tq, LANES), lambda b, c: (b * nc + c, 0)),
            kv_spec, kv_spec, kv_spec,
        ],
        out_specs=pl.BlockSpec((tq, NSA_HEADS * NSA_HEAD_DIM), lambda b, c: (b * nc + c, 0)),
        scratch_shapes=[
            pltpu.VMEM((NSA_ROW_W, LANES), bf16),
            pltpu.VMEM((seq_len // kc, NSA_ROW_W, kc), bf16),
            pltpu.VMEM((seq_len // LANES, NSA_ROW_W, LANES), bf16),
        ],
        compiler_params=_cparams("arbitrary", "arbitrary"),
        name="nsa_prompt",
    )(qa, gates, cmp_t, sel_t, win_t)


def _mla_prompt_body(qm_ref, lat_ref, o_ref, lat_sc, *, tq, kc, seq_len):
    c = pl.program_id(1)

    @pl.when(c == 0)
    def _():
        for ch in range(seq_len // kc):
            lat_sc[ch] = lat_ref[0, :, ch * kc:(ch + 1) * kc].astype(bf16)

    rows = MLA_HEADS * tq
    q = qm_ref[...].reshape(rows, MLA_LAT_W)
    t0 = c * tq
    tpos = t0 + lax.broadcasted_iota(jnp.int32, (tq, 1), 0)

    def full_step(ch, carry):
        k_t = lat_sc[ch]
        return _online_step(carry, _dot(q, k_t), k_t[:MLA_KV_RANK])

    n_full = t0 // kc
    carry = lax.fori_loop(0, n_full, full_step, _online_init(rows, MLA_KV_RANK))
    k_t = lat_sc[n_full]
    kpos = n_full * kc + lax.broadcasted_iota(jnp.int32, (1, kc), 1)
    bias = jnp.where(kpos <= tpos, 0.0, NEG_BIG)
    _, l_i, acc = _online_step(carry, _bias_heads(_dot(q, k_t), bias, MLA_HEADS), k_t[:MLA_KV_RANK])
    o = acc * (1.0 / l_i)
    for hh in range(MLA_HEADS):
        o_ref[:, hh * MLA_KV_RANK:(hh + 1) * MLA_KV_RANK] = o[hh * tq:(hh + 1) * tq].astype(o_ref.dtype)


def _mla_prompt(qm, lat_t, *, tq, kc):
    n_seq, _, seq_len = lat_t.shape
    n = n_seq * seq_len
    nc = seq_len // tq
    assert kc % tq == 0
    return pl.pallas_call(
        functools.partial(_mla_prompt_body, tq=tq, kc=kc, seq_len=seq_len),
        out_shape=jax.ShapeDtypeStruct((n, MLA_HEADS * MLA_KV_RANK), bf16),
        grid=(n_seq, nc),
        in_specs=[
            pl.BlockSpec((MLA_HEADS, tq, MLA_LAT_W), lambda b, c: (0, b * nc + c, 0)),
            pl.BlockSpec((1, MLA_LAT_W, seq_len), lambda b, c: (b, 0, 0)),
        ],
        out_specs=pl.BlockSpec((tq, MLA_HEADS * MLA_KV_RANK), lambda b, c: (b * nc + c, 0)),
        scratch_shapes=[pltpu.VMEM((seq_len // kc, MLA_LAT_W, kc), bf16)],
        compiler_params=_cparams("arbitrary", "arbitrary"),
        name="mla_prompt",
    )(qm, lat_t)


PAGES_PER_CHUNK = 16
CHUNK_KEYS = PAGES_PER_CHUNK * PAGE_SIZE


def _page_copy(cache_ref, layer, page, buf, slot, pi, sem):
    return pltpu.make_async_copy(cache_ref.at[page, layer], buf.at[slot, :, pl.ds(pi * LANES, LANES)], sem.at[slot])


def _start_chunk(pt_ref, cache_ref, layer, b, chunk, slot, buf, sem):
    for pi in range(PAGES_PER_CHUNK):
        page = pt_ref[b, chunk * PAGES_PER_CHUNK + pi]
        _page_copy(cache_ref, layer, page, buf, slot, pi, sem).start()


def _wait_chunk(cache_ref, layer, slot, buf, sem):
    for pi in range(PAGES_PER_CHUNK):
        _page_copy(cache_ref, layer, 0, buf, slot, pi, sem).wait()


def _stream_chunks(pt_ref, cache_ref, layer, n_chunks, buf, sem, fn):
    b = pl.program_id(0)
    n_seq = pl.num_programs(0)

    def slot_of(chunk):
        return chunk % 2 if n_chunks % 2 == 0 else (b * n_chunks + chunk) % 2

    @pl.when(b == 0)
    def _():
        _start_chunk(pt_ref, cache_ref, layer, 0, 0, 0, buf, sem)

    for chunk in range(n_chunks):
        slot = slot_of(chunk)
        if chunk + 1 < n_chunks:
            _start_chunk(pt_ref, cache_ref, layer, b, chunk + 1, 1 - slot, buf, sem)
        else:
            @pl.when(b + 1 < n_seq)
            def _():
                _start_chunk(pt_ref, cache_ref, layer, b + 1, 0, 1 - slot, buf, sem)
        _wait_chunk(cache_ref, layer, slot, buf, sem)
        fn(chunk, buf[slot])


def _cmp_means_body(pt_ref, cache_ref, o_ref, buf, sem, *, layer, n_chunks):
    nbc = CHUNK_KEYS // NSA_BLOCK
    avg = _block_avg_matrix(CHUNK_KEYS, nbc)

    def fn(chunk, slab):
        o_ref[0, :, chunk * nbc:(chunk + 1) * nbc] = _dot(slab.astype(bf16), avg)

    _stream_chunks(pt_ref, cache_ref, layer, n_chunks, buf, sem, fn)


def _cmp_means(page_table, cache_t, layer):
    n_dec, n_pages = page_table.shape
    n_chunks = n_pages // PAGES_PER_CHUNK
    nbp = n_pages * PAGE_SIZE // NSA_BLOCK
    return pl.pallas_call(
        functools.partial(_cmp_means_body, layer=layer, n_chunks=n_chunks),
        out_shape=jax.ShapeDtypeStruct((n_dec, NSA_ROW_W, nbp), f32),
        grid_spec=pltpu.PrefetchScalarGridSpec(
            num_scalar_prefetch=1,
            grid=(n_dec,),
            in_specs=[pl.BlockSpec(memory_space=pl.ANY)],
            out_specs=pl.BlockSpec((1, NSA_ROW_W, nbp), lambda b, pt: (b, 0, 0)),
            scratch_shapes=[pltpu.VMEM((2, NSA_ROW_W, CHUNK_KEYS), f32), pltpu.SemaphoreType.DMA((2,))],
        ),
        compiler_params=_cparams("arbitrary"),
        name="cmp_means",
    )(page_table, cache_t)


def _pick_past_blocks(imp, k_past):
    t, nb = imp.shape
    blk = lax.broadcasted_iota(jnp.int32, imp.shape, 1)
    v = jnp.where(blk == 0, jnp.inf, imp)
    v_cols = jnp.concatenate([v, jnp.zeros((LANES - t, nb), f32)], axis=0).T
    lower = lax.broadcasted_iota(jnp.int32, (nb, nb), 0) < lax.broadcasted_iota(jnp.int32, (nb, nb), 1)
    rows = []
    for tt in range(t):
        col = v_cols[:, tt:tt + 1]
        row = v[tt:tt + 1, :]
        beats = (col > row) | ((col == row) & lower)
        rank = jnp.sum(jnp.where(beats, 1.0, 0.0), axis=0, keepdims=True)
        rows.append(jnp.where(rank < k_past, 1.0, 0.0))
    return jnp.concatenate(rows, axis=0)


def _nsa_sample_body(pt_ref, qa_ref, gate_ref, kcm_ref, cmpn_ref, seln_ref, winn_ref, wst_ref, cache_ref,
                     o_ref, buf, sem, s_sc, kv_sc, *, layer, n_chunks, past_len, t_new):
    hd = NSA_HEAD_DIM
    rows = NSA_REP * t_new
    nbp = past_len // NSA_BLOCK
    nbc = CHUNK_KEYS // NSA_BLOCK
    wlen = wst_ref.shape[-1]
    tpos = past_len + lax.broadcasted_iota(jnp.int32, (t_new, 1), 0)
    t_row = lax.broadcasted_iota(jnp.int32, (t_new, 1), 0)
    t_col = lax.broadcasted_iota(jnp.int32, (1, t_new), 1)
    bias_new = jnp.where(t_col <= t_row, 0.0, NEG_BIG)
    gates = gate_ref[...]

    qs, chosen_b, o_cmps = [], [], []
    for g in range(NSA_GROUPS):
        qg = qa_ref[g * NSA_REP:(g + 1) * NSA_REP].reshape(rows, hd).astype(bf16)
        qs.append(qg)
        ks, vs = slice(g * hd, (g + 1) * hd), slice(NSA_KV_W + g * hd, NSA_KV_W + (g + 1) * hd)
        s = _dot(qg, kcm_ref[0, ks, :].astype(bf16))
        cmp_new = jnp.sum(cmpn_ref[0], axis=1, keepdims=True) * (1.0 / NSA_BLOCK)
        s_new = _dot(qg, cmp_new[ks].astype(bf16))
        new_complete = jnp.broadcast_to(((nbp + 1) * NSA_BLOCK - 1 <= tpos)[None],
                                        (NSA_REP, t_new, 1)).reshape(rows, 1)
        m = jnp.maximum(jnp.max(s, axis=-1, keepdims=True), jnp.where(new_complete, s_new, -jnp.inf))
        e = jnp.exp2(s - m)
        e_new = jnp.where(new_complete, jnp.exp2(s_new - m), 0.0)
        den = jnp.maximum(jnp.sum(e, axis=-1, keepdims=True) + e_new, 1e-30)
        p = e / den
        p_new = e_new / den
        o_cmp = _dot_nt(p.astype(bf16), kcm_ref[0, vs, :].astype(bf16))
        o_cmp = o_cmp + _dot_nt(p_new.astype(bf16), cmp_new[vs].astype(bf16))
        o_cmps.append(o_cmp)
        imp = jnp.sum(p.reshape(NSA_REP, t_new, nbp), axis=0)
        chosen = _pick_past_blocks(imp, min(NSA_TOP_K, nbp + 1) - 1)
        chosen_b.append(chosen.astype(bf16))

    expand = _block_expand_matrix(nbc, CHUNK_KEYS, 0)
    zq = jnp.zeros((rows, hd), bf16)
    q_bd = jnp.concatenate([jnp.concatenate([qs[0], zq], axis=1), jnp.concatenate([zq, qs[1]], axis=1)], axis=0)
    chosen_all = jnp.concatenate(chosen_b, axis=0)
    n_hd = NSA_GROUPS * NSA_REP

    def scores(chunk):
        lanes = slice(chunk * CHUNK_KEYS, (chunk + 1) * CHUNK_KEYS)
        hit = _dot(chosen_all[:, chunk * nbc:(chunk + 1) * nbc], expand)
        bias = jnp.where(hit > 0.5, 0.0, NEG_BIG).reshape(NSA_GROUPS, 1, t_new, CHUNK_KEYS)
        s = _dot(q_bd, kv_sc[:NSA_KV_W, lanes]).reshape(NSA_GROUPS, NSA_REP, t_new, CHUNK_KEYS)
        s_sc[:, lanes] = (s + bias).reshape(n_hd * t_new, CHUNK_KEYS)

    def fn(chunk, slab):
        kv_sc[:, chunk * CHUNK_KEYS:(chunk + 1) * CHUNK_KEYS] = slab.astype(bf16)
        if chunk > 0:
            scores(chunk - 1)

    _stream_chunks(pt_ref, cache_ref, layer, n_chunks, buf, sem, fn)
    scores(n_chunks - 1)

    wi = lax.broadcasted_iota(jnp.int32, (1, wlen), 1)
    bias_win = jnp.where(wi > t_row + (wlen - NSA_WINDOW), 0.0, NEG_BIG)
    s_past = s_sc[...]
    s_new = _bias_heads(_dot(q_bd, seln_ref[0, :NSA_KV_W, :].astype(bf16)), bias_new, n_hd)
    m = jnp.maximum(jnp.max(s_past, axis=-1, keepdims=True), jnp.max(s_new, axis=-1, keepdims=True))
    p_past = jnp.exp2(s_past - m)
    p_new = jnp.exp2(s_new - m)
    l_s = jnp.sum(p_past, axis=-1, keepdims=True) + jnp.sum(p_new, axis=-1, keepdims=True)
    acc_s = (_dot_nt(p_past.astype(bf16), kv_sc[NSA_KV_W:, :])
             + _dot_nt(p_new.astype(bf16), seln_ref[0, NSA_KV_W:, :].astype(bf16)))
    o_sel_all = acc_s * (1.0 / l_s)
    for g in range(NSA_GROUPS):
        ks, vs = slice(g * hd, (g + 1) * hd), slice(NSA_KV_W + g * hd, NSA_KV_W + (g + 1) * hd)
        qg = qs[g]
        o_sel = o_sel_all[g * rows:(g + 1) * rows, g * hd:(g + 1) * hd]
        s_w = _bias_heads(_dot(qg, wst_ref[0, 0, ks, :].astype(bf16)), bias_win, NSA_REP)
        s_n = _bias_heads(_dot(qg, winn_ref[0, ks, :].astype(bf16)), bias_new, NSA_REP)
        m = jnp.maximum(jnp.max(s_w, axis=-1, keepdims=True), jnp.max(s_n, axis=-1, keepdims=True))
        p_w = jnp.exp2(s_w - m)
        p_n = jnp.exp2(s_n - m)
        l_w = jnp.sum(p_w, axis=-1, keepdims=True) + jnp.sum(p_n, axis=-1, keepdims=True)
        acc_w = (_dot_nt(p_w.astype(bf16), wst_ref[0, 0, vs, :].astype(bf16))
                 + _dot_nt(p_n.astype(bf16), winn_ref[0, vs, :].astype(bf16)))
        o_win = acc_w * (1.0 / l_w)
        for r in range(NSA_REP):
            hh = g * NSA_REP + r
            sl = slice(r * t_new, (r + 1) * t_new)
            o = (gates[:, hh:hh + 1] * o_cmps[g][sl]
                 + gates[:, NSA_HEADS + hh:NSA_HEADS + hh + 1] * o_sel[sl]
                 + gates[:, 2 * NSA_HEADS + hh:2 * NSA_HEADS + hh + 1] * o_win[sl])
            o_ref[:, hh * hd:(hh + 1) * hd] = o.astype(o_ref.dtype)


def _nsa_sample(page_table, qa, gates, kcm, cmp_new, sel_new, win_new, win_state_t, cache_t, layer, t_new):
    n_dec, n_pages = page_table.shape
    n_chunks = n_pages // PAGES_PER_CHUNK
    past_len = n_pages * PAGE_SIZE
    nbp = past_len // NSA_BLOCK
    wlen = win_state_t.shape[-1]
    new_spec = pl.BlockSpec((1, NSA_ROW_W, t_new), lambda b, pt: (b, 0, 0))
    return pl.pallas_call(
        functools.partial(_nsa_sample_body, layer=layer, n_chunks=n_chunks, past_len=past_len, t_new=t_new),
        out_shape=jax.ShapeDtypeStruct((n_dec * t_new, NSA_HEADS * NSA_HEAD_DIM), f32),
        grid_spec=pltpu.PrefetchScalarGridSpec(
            num_scalar_prefetch=1,
            grid=(n_dec,),
            in_specs=[
                pl.BlockSpec((NSA_HEADS, t_new, NSA_HEAD_DIM), lambda b, pt: (0, b, 0)),
                pl.BlockSpec((t_new, LANES), lambda b, pt: (b, 0)),
                pl.BlockSpec((1, NSA_ROW_W, nbp), lambda b, pt: (b, 0, 0)),
                new_spec, new_spec, new_spec,
                pl.BlockSpec((1, 1, NSA_ROW_W, wlen), lambda b, pt: (layer, b, 0, 0)),
                pl.BlockSpec(memory_space=pl.ANY),
            ],
            out_specs=pl.BlockSpec((t_new, NSA_HEADS * NSA_HEAD_DIM), lambda b, pt: (b, 0)),
            scratch_shapes=[
                pltpu.VMEM((2, NSA_ROW_W, CHUNK_KEYS), f32),
                pltpu.SemaphoreType.DMA((2,)),
                pltpu.VMEM((NSA_HEADS * t_new, past_len), f32),
                pltpu.VMEM((NSA_ROW_W, past_len), bf16),
            ],
        ),
        compiler_params=_cparams("arbitrary"),
        name="nsa_sample",
    )(page_table, qa, gates, kcm, cmp_new, sel_new, win_new, win_state_t, cache_t)


def _mla_sample_body(pt_ref, qm_ref, latn_ref, cache_ref, o_ref, buf, sem, s_sc, k_sc, *, layer, n_chunks, t_new):
    rows = MLA_HEADS * t_new
    q = qm_ref[...].reshape(rows, MLA_LAT_W).astype(bf16)

    def scores(chunk):
        lanes = slice(chunk * CHUNK_KEYS, (chunk + 1) * CHUNK_KEYS)
        s_sc[:, lanes] = _dot(q, k_sc[:, lanes])

    def fn(chunk, slab):
        k_sc[:, chunk * CHUNK_KEYS:(chunk + 1) * CHUNK_KEYS] = slab.astype(bf16)
        if chunk > 0:
            scores(chunk - 1)

    _stream_chunks(pt_ref, cache_ref, layer, n_chunks, buf, sem, fn)
    scores(n_chunks - 1)

    t_row = lax.broadcasted_iota(jnp.int32, (t_new, 1), 0)
    t_col = lax.broadcasted_iota(jnp.int32, (1, t_new), 1)
    bias_new = jnp.where(t_col <= t_row, 0.0, NEG_BIG)
    k_new = latn_ref[0].astype(bf16)
    s_past = s_sc[...]
    s_new = _bias_heads(_dot(q, k_new), bias_new, MLA_HEADS)
    m = jnp.maximum(jnp.max(s_past, axis=-1, keepdims=True), jnp.max(s_new, axis=-1, keepdims=True))
    p_past = jnp.exp2(s_past - m)
    p_new = jnp.exp2(s_new - m)
    l_i = jnp.sum(p_past, axis=-1, keepdims=True) + jnp.sum(p_new, axis=-1, keepdims=True)
    acc = (_dot_nt(p_past.astype(bf16), k_sc[:MLA_KV_RANK, :])
           + _dot_nt(p_new.astype(bf16), k_new[:MLA_KV_RANK]))
    o = acc * (1.0 / l_i)
    for hh in range(MLA_HEADS):
        o_ref[:, hh * MLA_KV_RANK:(hh + 1) * MLA_KV_RANK] = o[hh * t_new:(hh + 1) * t_new].astype(o_ref.dtype)


def _mla_sample(page_table, qm, lat_new, cache_t, layer, t_new):
    n_dec, n_pages = page_table.shape
    n_chunks = n_pages // PAGES_PER_CHUNK
    past_len = n_pages * PAGE_SIZE
    return pl.pallas_call(
        functools.partial(_mla_sample_body, layer=layer, n_chunks=n_chunks, t_new=t_new),
        out_shape=jax.ShapeDtypeStruct((n_dec * t_new, MLA_HEADS * MLA_KV_RANK), f32),
        grid_spec=pltpu.PrefetchScalarGridSpec(
            num_scalar_prefetch=1,
            grid=(n_dec,),
            in_specs=[
                pl.BlockSpec((MLA_HEADS, t_new, MLA_LAT_W), lambda b, pt: (0, b, 0)),
                pl.BlockSpec((1, MLA_LAT_W, t_new), lambda b, pt: (b, 0, 0)),
                pl.BlockSpec(memory_space=pl.ANY),
            ],
            out_specs=pl.BlockSpec((t_new, MLA_HEADS * MLA_KV_RANK), lambda b, pt: (b, 0)),
            scratch_shapes=[
                pltpu.VMEM((2, MLA_LAT_W, CHUNK_KEYS), f32),
                pltpu.SemaphoreType.DMA((2,)),
                pltpu.VMEM((MLA_HEADS * t_new, past_len), f32),
                pltpu.VMEM((MLA_LAT_W, past_len), bf16),
            ],
        ),
        compiler_params=_cparams("arbitrary"),
        name="mla_sample",
    )(page_table, qm, lat_new, cache_t)


def _rope_tables(pos):
    pos = pos.astype(f32)

    def cs(dim):
        inv = ROPE_THETA ** (-jnp.arange(0, dim, 2, dtype=f32) / dim)
        ang = pos[:, None] * inv[None, :]
        return jnp.cos(ang), jnp.sin(ang)

    ca, sa = cs(NSA_HEAD_DIM)
    cb, sb = cs(MLA_ROPE_DIM)
    rep_a = LANES // NSA_HEAD_DIM
    rep_b = LANES // MLA_ROPE_DIM
    return {
        "ca": jnp.tile(ca, (1, 2 * rep_a)), "sa": jnp.tile(jnp.concatenate([-sa, sa], axis=1), (1, rep_a)),
        "cb": jnp.tile(cb, (1, 2 * rep_b)), "sb": jnp.tile(jnp.concatenate([-sb, sb], axis=1), (1, rep_b)),
        "cat": ca.T, "sat": sa.T, "cbt": cb.T, "sbt": sb.T,
    }


def _layer_weights(l, p):
    d = p["w_in"].shape[1]
    w_in = p["w_in"][l]
    nq = NSA_HEADS * NSA_HEAD_DIM
    sizes = (nq, 6 * NSA_KV_W, N_GATES, MLA_Q_RANK, MLA_KV_RANK, MLA_ROPE_DIM, 2 * d)
    o = np.cumsum((0,) + sizes)
    w_q, w_kv, w_gate, w_cq, w_ckv, w_kr, w_merge = (w_in[:, o[i]:o[i + 1]] for i in range(7))
    w_gate = w_gate.reshape(d, NSA_HEADS, 3).transpose(0, 2, 1).reshape(d, N_GATES)
    w_gate = jnp.pad(w_gate, ((0, 0), (0, LANES - N_GATES)))
    ws = jnp.concatenate([w_q, w_cq, w_gate], axis=1).astype(bf16)
    wt = jnp.concatenate([w_kv, w_ckv, w_kr], axis=1).T.astype(bf16)
    w_uq = p["w_uq"][l].reshape(MLA_Q_RANK, MLA_HEADS, MLA_NOPE_DIM + MLA_ROPE_DIM)
    wuq = jnp.concatenate([w_uq[:, :, :MLA_NOPE_DIM].reshape(MLA_Q_RANK, -1),
                           w_uq[:, :, MLA_NOPE_DIM:].reshape(MLA_Q_RANK, -1)], axis=1).astype(bf16)
    eye = jnp.eye(MLA_HEADS, dtype=f32)
    wuk_bd = jnp.einsum("chn,hk->hnkc", p["w_uk"][l], eye).reshape(
        MLA_HEADS * MLA_NOPE_DIM, MLA_HEADS * MLA_KV_RANK).astype(bf16)
    wuv_bd = jnp.einsum("chv,hk->hckv", p["w_uv"][l], eye).reshape(
        MLA_HEADS * MLA_KV_RANK, MLA_HEADS * MLA_V_DIM).astype(bf16)
    return {
        "g_ffn_a": p["g_ffn_a"][l][None], "w13_a": p["w13_a"][l].astype(bf16), "w2_a": p["w2_a"][l].astype(bf16),
        "g_ffn_b": p["g_ffn_b"][l][None], "w13_b": p["w13_b"][l].astype(bf16), "w2_b": p["w2_b"][l].astype(bf16),
        "g_mix": p["g_mix"][l][None], "ws": ws, "wt": wt, "wm": w_merge.astype(bf16),
        "g_cq": p["g_cq"][l][None], "wuq": wuq, "wuk_bd": wuk_bd, "g_ckv": p["g_ckv"][l][:, None],
        "wuv_bd": wuv_bd, "w_o_nsa": p["w_o_nsa"][l].astype(bf16), "w_o_mla": p["w_o_mla"][l].astype(bf16),
        "w_out": p["w_out"][l].astype(bf16),
    }


def _tile(n, want):
    t = min(n, want)
    assert n % t == 0, (n, want)
    return t


FFN_SPLIT = 2


def _ffn_pair(x, wl, which, gf, final_norm):
    n = x.shape[0]
    dff = wl["w2_" + which].shape[0]
    n_split = FFN_SPLIT if dff % (FFN_SPLIT * LANES) == 0 else 1
    return _ffn(x, wl["g_ffn_" + which], wl["w13_" + which], wl["w2_" + which], gf,
                final_norm=final_norm, tm=_tile(n, 512), n_split=n_split)


def _prompt_trunk(x_prompt, wls, g_final):
    bsz, seq_len, d = x_prompt.shape
    assert seq_len % LANES == 0 and seq_len % NSA_BLOCK == 0
    depth = len(wls)
    n = bsz * seq_len
    x = x_prompt.reshape(n, d)
    tabs = _rope_tables(jnp.arange(seq_len, dtype=jnp.int32))
    tm = _tile(seq_len, 512)
    tq = LANES
    kc = _tile(seq_len, 512)
    cmp_rows, sel_rows, lat_rows, win_rows = [], [], [], []
    for l, wl in enumerate(wls):
        x = _ffn_pair(x, wl, "a", g_final, False)
        qa, gates, qm, cmp_t, sel_t, win_t, lat_t = _proj(x, seq_len, wl, tabs, tm=tm, q_dtype=bf16)
        o_nsa = _nsa_prompt(qa, gates, cmp_t, sel_t, win_t, tq=tq, kc=kc)
        o_lat = _mla_prompt(qm, lat_t, tq=tq, kc=kc)
        x = _merge(x, o_nsa, o_lat, wl, tm=tm)
        x = _ffn_pair(x, wl, "b", g_final, l == depth - 1)
        cmp_rows.append(cmp_t)
        sel_rows.append(sel_t)
        lat_rows.append(lat_t)
        win_rows.append(win_t)
    return x.reshape(bsz, seq_len, d), cmp_rows, sel_rows, lat_rows, win_rows


def _per_seq(rows_t, n_dec, t_new):
    f = rows_t.shape[1]
    return jnp.transpose(rows_t[0].reshape(f, n_dec, t_new), (1, 0, 2))


def _sample_trunk(x_sample, cmp_c, sel_c, mla_c, win_state_t, page_table, wls, g_final):
    n_dec, t_new, d = x_sample.shape
    n_pages = page_table.shape[1]
    assert t_new < NSA_BLOCK and n_pages % PAGES_PER_CHUNK == 0 and PAGE_SIZE % NSA_BLOCK == 0
    depth = len(wls)
    n = n_dec * t_new
    past_len = n_pages * PAGE_SIZE
    x = x_sample.reshape(n, d)
    pos = past_len + jnp.arange(t_new, dtype=jnp.int32)
    tabs = _rope_tables(jnp.tile(pos, n_dec))
    tm = _tile(n, 512)
    cmp_rows, sel_rows, lat_rows, win_rows = [], [], [], []
    for l, wl in enumerate(wls):
        x = _ffn_pair(x, wl, "a", g_final, False)
        qa, gates, qm, cmp_t, sel_t, win_t, lat_t = _proj(x, n, wl, tabs, tm=tm, q_dtype=f32)
        cmp_n, sel_n, win_n, lat_n = (_per_seq(a, n_dec, t_new) for a in (cmp_t, sel_t, win_t, lat_t))
        kcm = _cmp_means(page_table, cmp_c, l)
        o_nsa = _nsa_sample(page_table, qa, gates, kcm, cmp_n, sel_n, win_n, win_state_t, sel_c, l, t_new)
        o_lat = _mla_sample(page_table, qm, lat_n, mla_c, l, t_new)
        x = _merge(x, o_nsa, o_lat, wl, tm=tm)
        x = _ffn_pair(x, wl, "b", g_final, l == depth - 1)
        cmp_rows.append(cmp_n)
        sel_rows.append(sel_n)
        lat_rows.append(lat_n)
        win_rows.append(jnp.concatenate([win_state_t[l][:, :, t_new:], win_n], axis=-1))
    return x.reshape(n_dec, t_new, d), cmp_rows, sel_rows, lat_rows, win_rows


def _kv_rows_out(rows_t):
    t = jnp.stack(rows_t, axis=1)
    b, depth, _, s = t.shape
    t = t.reshape(b, depth, 2, NSA_GROUPS, NSA_HEAD_DIM, s)
    return jnp.transpose(t, (0, 5, 1, 2, 3, 4))


def _lat_rows_out(rows_t):
    return jnp.transpose(jnp.stack(rows_t, axis=1), (0, 3, 1, 2))


def _win_out(rows_t):
    t = jnp.stack(rows_t, axis=0)
    depth, b, _, w = t.shape
    t = t.reshape(depth, b, 2, NSA_GROUPS, NSA_HEAD_DIM, w)
    return jnp.transpose(t, (0, 1, 5, 2, 3, 4))


def _last_rows(rows_t, wlen):
    s = rows_t.shape[-1]
    if s >= wlen:
        return rows_t[:, :, s - wlen:]
    return jnp.pad(rows_t, ((0, 0), (0, 0), (wlen - s, 0)))


def kernel(x_prompt, x_sample, cache_nsa_cmp, cache_nsa_sel, cache_mla, state_nsa_win, page_table,
           g_ffn_a, w13_a, w2_a, g_mix, w_in, g_cq, w_uq, g_ckv, w_uk, w_uv,
           w_o_nsa, w_o_mla, w_out, g_ffn_b, w13_b, w2_b, g_final):
    p = dict(g_ffn_a=g_ffn_a, w13_a=w13_a, w2_a=w2_a, g_mix=g_mix, w_in=w_in, g_cq=g_cq, w_uq=w_uq,
             g_ckv=g_ckv, w_uk=w_uk, w_uv=w_uv, w_o_nsa=w_o_nsa, w_o_mla=w_o_mla, w_out=w_out,
             g_ffn_b=g_ffn_b, w13_b=w13_b, w2_b=w2_b)
    assert cache_nsa_cmp.shape[1] == PAGE_SIZE
    n_pool, _, depth = cache_nsa_cmp.shape[:3]
    wlen = state_nsa_win.shape[2]
    wls = [_layer_weights(l, p) for l in range(depth)]
    gf = g_final[None]
    cmp_c = jnp.transpose(cache_nsa_cmp, (0, 2, 3, 4, 5, 1)).reshape(n_pool, depth, NSA_ROW_W, PAGE_SIZE)
    sel_c = jnp.transpose(cache_nsa_sel, (0, 2, 3, 4, 5, 1)).reshape(n_pool, depth, NSA_ROW_W, PAGE_SIZE)
    mla_c = jnp.transpose(cache_mla, (0, 2, 3, 1))
    win_state_t = jnp.transpose(state_nsa_win, (0, 1, 3, 4, 5, 2)).reshape(
        depth, state_nsa_win.shape[1], NSA_ROW_W, wlen)

    y_p, cmp_p, sel_p, lat_p, win_p = _prompt_trunk(x_prompt, wls, gf)
    y_s, cmp_s, sel_s, lat_s, win_s = _sample_trunk(x_sample, cmp_c, sel_c, mla_c, win_state_t, page_table, wls, gf)
    win_p = [_last_rows(w, wlen) for w in win_p]
    return (y_p, y_s,
            _kv_rows_out(cmp_p), _kv_rows_out(cmp_s),
            _kv_rows_out(sel_p), _kv_rows_out(sel_s),
            _lat_rows_out(lat_p), _lat_rows_out(lat_s),
            _win_out(win_p), _win_out(win_s))
```

```python
import functools
import math

import jax
import jax.numpy as jnp
import numpy as np
from jax import lax
from jax.experimental import pallas as pl
from jax.experimental.pallas import tpu as pltpu

f32 = jnp.float32
bf16 = jnp.bfloat16

NSA_HEADS = 8
NSA_GROUPS = 2
NSA_HEAD_DIM = 64
NSA_REP = NSA_HEADS // NSA_GROUPS
NSA_BLOCK = 64
NSA_TOP_K = 16
NSA_WINDOW = 512
MLA_HEADS = 8
MLA_Q_RANK = 256
MLA_KV_RANK = 128
MLA_NOPE_DIM = 64
MLA_ROPE_DIM = 32
MLA_V_DIM = 64
MLA_LAT_W = MLA_KV_RANK + MLA_ROPE_DIM
ROPE_THETA = 10000.0
RMS_EPS = 1e-6
PAGE_SIZE = 128
NSA_KV_W = NSA_GROUPS * NSA_HEAD_DIM
NSA_ROW_W = 2 * NSA_KV_W
N_GATES = 3 * NSA_HEADS

LANES = 128
VMEM_LIMIT_BYTES = 56 * 1024 * 1024

NEG_BIG = -1e30
LOG2E = math.log2(math.e)


def _cparams(*sem):
    return pltpu.CompilerParams(dimension_semantics=sem, vmem_limit_bytes=VMEM_LIMIT_BYTES)


def _rms(x, g):
    ms = jnp.mean(x * x, axis=-1, keepdims=True)
    return x * lax.rsqrt(ms + RMS_EPS) * g


def _dot(a, b):
    return jnp.dot(a, b, preferred_element_type=f32)


def _dot_nt(a, b):
    return lax.dot_general(a, b, (((1,), (1,)), ((), ())), preferred_element_type=f32)


def _ffn_body(x_ref, g_ref, w1_ref, w3_ref, w2_ref, gf_ref, o_ref, *, final_norm, n_split):
    x = x_ref[...]
    h = _rms(x, g_ref[...]).astype(bf16)
    dff = w2_ref.shape[0]
    tf = dff // n_split
    acc = None
    for j in range(n_split):
        a = _dot(h, w1_ref[:, j * tf:(j + 1) * tf])
        b = _dot(h, w3_ref[:, j * tf:(j + 1) * tf])
        u = (a * jax.nn.sigmoid(a) * b).astype(bf16)
        part = _dot(u, w2_ref[j * tf:(j + 1) * tf, :])
        acc = part if acc is None else acc + part
    y = x + 0.5 * acc
    if final_norm:
        y = _rms(y, gf_ref[...])
    o_ref[...] = y


def _ffn(x, g, w13, w2, gf, *, final_norm, tm, n_split):
    n, d = x.shape
    dff = w2.shape[0]
    resident = pl.Buffered(1)
    return pl.pallas_call(
        functools.partial(_ffn_body, final_norm=final_norm, n_split=n_split),
        out_shape=jax.ShapeDtypeStruct((n, d), f32),
        grid=(n // tm,),
        in_specs=[
            pl.BlockSpec((tm, d), lambda i: (i, 0)),
            pl.BlockSpec((1, d), lambda i: (0, 0)),
            pl.BlockSpec((d, dff), lambda i: (0, 0), pipeline_mode=resident),
            pl.BlockSpec((d, dff), lambda i: (0, 1), pipeline_mode=resident),
            pl.BlockSpec((dff, d), lambda i: (0, 0), pipeline_mode=resident),
            pl.BlockSpec((1, d), lambda i: (0, 0)),
        ],
        out_specs=pl.BlockSpec((tm, d), lambda i: (i, 0)),
        compiler_params=_cparams("arbitrary"),
        name="ffn",
    )(x, g, w13, w13, w2, gf)


def _rope_lanes(x, cos, sin_signed, half):
    w = x.shape[-1]
    lane = lax.broadcasted_iota(jnp.int32, x.shape, x.ndim - 1)
    first = (lane % (2 * half)) < half
    rot = jnp.where(first, pltpu.roll(x, w - half, x.ndim - 1), pltpu.roll(x, half, x.ndim - 1))
    return x * cos + rot * sin_signed


def _rope_rows(x, cos, sin, half):
    x1, x2 = x[:half], x[half:]
    return jnp.concatenate([x1 * cos - x2 * sin, x2 * cos + x1 * sin], axis=0)


def _proj_body(x_ref, g_ref, ws_ref, wt_ref, gcq_ref, wuq_ref, wuk_ref, gckv_ref,
               ca_ref, sa_ref, cb_ref, sb_ref, cat_ref, sat_ref, cbt_ref, sbt_ref,
               qa_ref, gate_ref, qm_ref, cmp_ref, sel_ref, win_ref, lat_ref):
    h = _rms(x_ref[...], g_ref[...]).astype(bf16)
    ps = _dot(h, ws_ref[...])
    pt = _dot_nt(wt_ref[...], h)
    nq = NSA_HEADS * NSA_HEAD_DIM
    scale_a = NSA_HEAD_DIM ** -0.5 * LOG2E
    ca, sa = ca_ref[...], sa_ref[...]
    for c in range(nq // LANES):
        q = _rope_lanes(ps[:, c * LANES:(c + 1) * LANES], ca, sa, NSA_HEAD_DIM // 2) * scale_a
        per = LANES // NSA_HEAD_DIM
        for r in range(per):
            qa_ref[c * per + r] = q[:, r * NSA_HEAD_DIM:(r + 1) * NSA_HEAD_DIM].astype(qa_ref.dtype)
    gate_ref[...] = jax.nn.sigmoid(ps[:, nq + MLA_Q_RANK:nq + MLA_Q_RANK + LANES])
    cqn = _rms(ps[:, nq:nq + MLA_Q_RANK], gcq_ref[...]).astype(bf16)
    qb = _dot(cqn, wuq_ref[...])
    n_nope = MLA_HEADS * MLA_NOPE_DIM
    scale_b = (MLA_NOPE_DIM + MLA_ROPE_DIM) ** -0.5 * LOG2E
    q_lat = _dot(qb[:, :n_nope].astype(bf16), wuk_ref[...]) * scale_b
    cb, sb = cb_ref[...], sb_ref[...]
    per = LANES // MLA_ROPE_DIM
    for c in range(MLA_HEADS // per):
        lo = n_nope + c * LANES
        q_rope = _rope_lanes(qb[:, lo:lo + LANES], cb, sb, MLA_ROPE_DIM // 2) * scale_b
        for r in range(per):
            qm_ref[c * per + r, :, MLA_KV_RANK:MLA_LAT_W] = (
                q_rope[:, r * MLA_ROPE_DIM:(r + 1) * MLA_ROPE_DIM].astype(qm_ref.dtype))
    for hh in range(MLA_HEADS):
        qm_ref[hh, :, 0:MLA_KV_RANK] = q_lat[:, hh * MLA_KV_RANK:(hh + 1) * MLA_KV_RANK].astype(qm_ref.dtype)
    cat, sat = cat_ref[...], sat_ref[...]
    for br, o_ref in enumerate((cmp_ref, sel_ref, win_ref)):
        base = br * NSA_ROW_W
        for gi in range(NSA_GROUPS):
            lo = base + gi * NSA_HEAD_DIM
            o_ref[0, gi * NSA_HEAD_DIM:(gi + 1) * NSA_HEAD_DIM, :] = _rope_rows(
                pt[lo:lo + NSA_HEAD_DIM], cat, sat, NSA_HEAD_DIM // 2)
        o_ref[0, NSA_KV_W:NSA_ROW_W, :] = pt[base + NSA_KV_W:base + NSA_ROW_W]
    base = 3 * NSA_ROW_W
    ckv = pt[base:base + MLA_KV_RANK]
    ms = jnp.mean(ckv * ckv, axis=0, keepdims=True)
    lat_ref[0, 0:MLA_KV_RANK, :] = ckv * lax.rsqrt(ms + RMS_EPS) * gckv_ref[...]
    lat_ref[0, MLA_KV_RANK:MLA_LAT_W, :] = _rope_rows(
        pt[base + MLA_KV_RANK:base + MLA_LAT_W], cbt_ref[...], sbt_ref[...], MLA_ROPE_DIM // 2)


def _proj(x, seq_len, wl, tabs, *, tm, q_dtype):
    n, d = x.shape
    nper = seq_len // tm
    n_seq = n // seq_len
    tok = lambda i: (i, 0)
    per = lambda i: (i % nper, 0)
    per_t = lambda i: (0, i % nper)
    const = lambda i: (0, 0)
    kv_map = lambda i: (i // nper, 0, i % nper)
    ws, wt, wuq, wuk = wl["ws"], wl["wt"], wl["wuq"], wl["wuk_bd"]
    out_shape = (
        jax.ShapeDtypeStruct((NSA_HEADS, n, NSA_HEAD_DIM), q_dtype),
        jax.ShapeDtypeStruct((n, LANES), f32),
        jax.ShapeDtypeStruct((MLA_HEADS, n, MLA_LAT_W), q_dtype),
        jax.ShapeDtypeStruct((n_seq, NSA_ROW_W, seq_len), f32),
        jax.ShapeDtypeStruct((n_seq, NSA_ROW_W, seq_len), f32),
        jax.ShapeDtypeStruct((n_seq, NSA_ROW_W, seq_len), f32),
        jax.ShapeDtypeStruct((n_seq, MLA_LAT_W, seq_len), f32),
    )
    out_specs = (
        pl.BlockSpec((NSA_HEADS, tm, NSA_HEAD_DIM), lambda i: (0, i, 0)),
        pl.BlockSpec((tm, LANES), tok),
        pl.BlockSpec((MLA_HEADS, tm, MLA_LAT_W), lambda i: (0, i, 0)),
        pl.BlockSpec((1, NSA_ROW_W, tm), kv_map),
        pl.BlockSpec((1, NSA_ROW_W, tm), kv_map),
        pl.BlockSpec((1, NSA_ROW_W, tm), kv_map),
        pl.BlockSpec((1, MLA_LAT_W, tm), kv_map),
    )
    in_specs = [
        pl.BlockSpec((tm, d), tok),
        pl.BlockSpec((1, d), const),
        pl.BlockSpec(ws.shape, const),
        pl.BlockSpec(wt.shape, const),
        pl.BlockSpec((1, MLA_Q_RANK), const),
        pl.BlockSpec(wuq.shape, const),
        pl.BlockSpec(wuk.shape, const),
        pl.BlockSpec((MLA_KV_RANK, 1), const),
        pl.BlockSpec((tm, LANES), per), pl.BlockSpec((tm, LANES), per),
        pl.BlockSpec((tm, LANES), per), pl.BlockSpec((tm, LANES), per),
        pl.BlockSpec((NSA_HEAD_DIM // 2, tm), per_t), pl.BlockSpec((NSA_HEAD_DIM // 2, tm), per_t),
        pl.BlockSpec((MLA_ROPE_DIM // 2, tm), per_t), pl.BlockSpec((MLA_ROPE_DIM // 2, tm), per_t),
    ]
    return pl.pallas_call(
        _proj_body,
        out_shape=out_shape,
        grid=(n // tm,),
        in_specs=in_specs,
        out_specs=out_specs,
        compiler_params=_cparams("arbitrary"),
        name="proj",
    )(x, wl["g_mix"], ws, wt, wl["g_cq"], wuq, wuk, wl["g_ckv"],
      tabs["ca"], tabs["sa"], tabs["cb"], tabs["sb"], tabs["cat"], tabs["sat"], tabs["cbt"], tabs["sbt"])


def _merge_body(x_ref, g_ref, wm_ref, on_ref, ol_ref, won_ref, wuv_ref, wom_ref, wout_ref, o_ref):
    x = x_ref[...]
    d = x.shape[-1]
    h = _rms(x, g_ref[...]).astype(bf16)
    mg = _dot(h, wm_ref[...])
    ga = jax.nn.sigmoid(mg[:, :d])
    gb = jax.nn.sigmoid(mg[:, d:])
    a = _dot(on_ref[...].astype(bf16), won_ref[...])
    om = _dot(ol_ref[...].astype(bf16), wuv_ref[...]).astype(bf16)
    b = _dot(om, wom_ref[...])
    m = (ga * a + gb * b).astype(bf16)
    o_ref[...] = x + _dot(m, wout_ref[...])


def _merge(x, o_nsa, o_lat, wl, *, tm):
    n, d = x.shape
    tok = lambda i: (i, 0)
    const = lambda i: (0, 0)
    ws = [wl["wm"], wl["w_o_nsa"], wl["wuv_bd"], wl["w_o_mla"], wl["w_out"]]
    return pl.pallas_call(
        _merge_body,
        out_shape=jax.ShapeDtypeStruct((n, d), f32),
        grid=(n // tm,),
        in_specs=[
            pl.BlockSpec((tm, d), tok),
            pl.BlockSpec((1, d), const),
            pl.BlockSpec(ws[0].shape, const),
            pl.BlockSpec((tm, o_nsa.shape[1]), tok),
            pl.BlockSpec((tm, o_lat.shape[1]), tok),
            pl.BlockSpec(ws[1].shape, const),
            pl.BlockSpec(ws[2].shape, const),
            pl.BlockSpec(ws[3].shape, const),
            pl.BlockSpec(ws[4].shape, const),
        ],
        out_specs=pl.BlockSpec((tm, d), tok),
        compiler_params=_cparams("arbitrary"),
        name="merge",
    )(x, wl["g_mix"], ws[0], o_nsa, o_lat, ws[1], ws[2], ws[3], ws[4])


def _online_step(carry, s, v_t):
    m_i, l_i, acc = carry
    m_new = jnp.maximum(m_i, jnp.max(s, axis=-1, keepdims=True))
    p = jnp.exp2(s - m_new)
    alpha = jnp.exp2(m_i - m_new)
    l_new = alpha * l_i + jnp.sum(p, axis=-1, keepdims=True)
    acc_new = alpha * acc + _dot_nt(p.astype(bf16), v_t)
    return m_new, l_new, acc_new


def _online_init(rows, dv):
    return (jnp.full((rows, 1), NEG_BIG, f32), jnp.zeros((rows, 1), f32), jnp.zeros((rows, dv), f32))


def _bias_heads(s, bias, n_heads):
    t, k = bias.shape
    return (s.reshape(n_heads, t, k) + bias[None]).reshape(n_heads * t, k)


def _hi_lo(x):
    hi = x.astype(bf16)
    return hi, (x - hi.astype(f32)).astype(bf16)


def _block_avg_matrix(n_keys, n_cols):
    row = lax.broadcasted_iota(jnp.int32, (n_keys, n_cols), 0)
    col = lax.broadcasted_iota(jnp.int32, (n_keys, n_cols), 1)
    return jnp.where(row // NSA_BLOCK == col, 1.0 / NSA_BLOCK, 0.0).astype(bf16)


def _block_expand_matrix(n_rows, n_keys, key0):
    eb = lax.broadcasted_iota(jnp.int32, (n_rows, n_keys), 0)
    ek = key0 + lax.broadcasted_iota(jnp.int32, (n_rows, n_keys), 1)
    return jnp.where(ek // NSA_BLOCK == eb, 1.0, 0.0).astype(bf16)


def _nsa_prompt_body(qa_ref, gate_ref, cmp_ref, sel_ref, win_ref, o_ref, kcm_sc, sel_sc, win_sc,
                     *, tq, kc, seq_len):
    c = pl.program_id(1)
    nb = seq_len // NSA_BLOCK
    hd = NSA_HEAD_DIM
    rows = NSA_REP * tq

    @pl.when(c == 0)
    def _():
        avg = _block_avg_matrix(seq_len, LANES)
        hi, lo = _hi_lo(cmp_ref[0])
        kcm_sc[...] = (_dot(hi, avg) + _dot(lo, avg)).astype(bf16)
        for ch in range(seq_len // kc):
            sel_sc[ch] = sel_ref[0, :, ch * kc:(ch + 1) * kc].astype(bf16)
        for ch in range(seq_len // LANES):
            win_sc[ch] = win_ref[0, :, ch * LANES:(ch + 1) * LANES].astype(bf16)

    t0 = c * tq
    tpos = t0 + lax.broadcasted_iota(jnp.int32, (tq, 1), 0)
    blk = lax.broadcasted_iota(jnp.int32, (1, LANES), 1)
    complete = ((blk + 1) * NSA_BLOCK - 1 <= tpos) & (blk < nb)
    nbr = -(-nb // 8) * 8
    blk_t = lax.broadcasted_iota(jnp.int32, (nbr, tq), 0)
    tpos_t = t0 + lax.broadcasted_iota(jnp.int32, (1, tq), 1)
    complete_t = ((blk_t + 1) * NSA_BLOCK - 1 <= tpos_t) & (blk_t < nb)
    forced_t = ((blk_t == tpos_t // NSA_BLOCK) | (blk_t == 0)) & (blk_t < nb)
    gates = gate_ref[...]
    n_sel_chunks = (t0 + tq + kc - 1) // kc
    n_band = NSA_WINDOW // LANES + tq // LANES
    band0 = c - (n_band - 1)
    kpos_w = band0 * LANES + lax.broadcasted_iota(jnp.int32, (1, n_band * LANES), 1)
    bias_w = jnp.where((kpos_w >= 0) & (kpos_w <= tpos) & (kpos_w > tpos - NSA_WINDOW), 0.0, NEG_BIG)

    for g in range(NSA_GROUPS):
        ks, vs = slice(g * hd, (g + 1) * hd), slice(NSA_KV_W + g * hd, NSA_KV_W + (g + 1) * hd)
        qg = qa_ref[g * NSA_REP:(g + 1) * NSA_REP].reshape(rows, hd)
        s = _dot(qg, kcm_sc[ks, :])
        vc_t = kcm_sc[vs, :]
        imp = jnp.zeros((tq, LANES), f32)
        o_cmp = []
        for r in range(NSA_REP):
            sr = s[r * tq:(r + 1) * tq]
            m = jnp.max(jnp.where(complete, sr, -jnp.inf), axis=-1, keepdims=True)
            m = jnp.where(m == -jnp.inf, 0.0, m)
            e = jnp.where(complete, jnp.exp2(sr - m), 0.0)
            p = e / jnp.maximum(jnp.sum(e, axis=-1, keepdims=True), 1e-30)
            imp = imp + p
            o_cmp.append(_dot_nt(p.astype(bf16), vc_t))
        v = jnp.where(forced_t, jnp.inf, jnp.where(complete_t, imp.T[:nbr], -jnp.inf))
        rank = jnp.zeros((nbr, tq), jnp.int32)
        for i in range(nb):
            row = v[i:i + 1, :]
            beats = (row > v) | ((row == v) & (blk_t > i))
            rank = rank + beats.astype(jnp.int32)
        chosen_t = jnp.where((rank < min(NSA_TOP_K, nb)) & (blk_t < nb), 1.0, 0.0)
        chosen = jnp.concatenate([chosen_t, jnp.zeros((LANES - nbr, tq), f32)], axis=0).T.astype(bf16)

        def sel_step(ch, carry):
            k_t = sel_sc[ch, ks, :]
            v_t = sel_sc[ch, vs, :]
            kpos = ch * kc + lax.broadcasted_iota(jnp.int32, (1, kc), 1)
            hit = _dot(chosen, _block_expand_matrix(LANES, kc, ch * kc))
            bias = jnp.where((hit > 0.5) & (kpos <= tpos), 0.0, NEG_BIG)
            return _online_step(carry, _bias_heads(_dot(qg, k_t), bias, NSA_REP), v_t)

        carry = lax.fori_loop(0, n_sel_chunks // 2, lambda i, cr: sel_step(2 * i + 1, sel_step(2 * i, cr)),
                              _online_init(rows, hd))
        _, l_s, acc_s = lax.cond(n_sel_chunks % 2 == 1, lambda cr: sel_step(n_sel_chunks - 1, cr),
                                 lambda cr: cr, carry)

        idc = [jnp.maximum(band0 + j, 0) for j in range(n_band)]
        k_band = jnp.concatenate([win_sc[i, ks, :] for i in idc], axis=1)
        v_band = jnp.concatenate([win_sc[i, vs, :] for i in idc], axis=1)
        sw = _bias_heads(_dot(qg, k_band), bias_w, NSA_REP)
        pw = jnp.exp2(sw - jnp.max(sw, axis=-1, keepdims=True))
        l_w = jnp.sum(pw, axis=-1, keepdims=True)
        acc_w = _dot_nt(pw.astype(bf16), v_band)

        r_s = 1.0 / l_s
        r_w = 1.0 / l_w
        for r in range(NSA_REP):
            hh = g * NSA_REP + r
            sl = slice(r * tq, (r + 1) * tq)

            def gate(branch):
                k = branch * NSA_HEADS + hh
                return gates[:, k:k + 1]

            o = gate(0) * o_cmp[r] + gate(1) * (acc_s[sl] * r_s[sl]) + gate(2) * (acc_w[sl] * r_w[sl])
            o_ref[:, hh * hd:(hh + 1) * hd] = o.astype(o_ref.dtype)


def _nsa_prompt(qa, gates, cmp_t, sel_t, win_t, *, tq, kc):
    n_seq, _, seq_len = cmp_t.shape
    n = n_seq * seq_len
    nc = seq_len // tq
    assert seq_len // NSA_BLOCK <= LANES and tq % LANES == 0
    kv_spec = pl.BlockSpec((1, NSA_ROW_W, seq_len), lambda b, c: (b, 0, 0))
    return pl.pallas_call(
        functools.partial(_nsa_prompt_body, tq=tq, kc=kc, seq_len=seq_len),
        out_shape=jax.ShapeDtypeStruct((n, NSA_HEADS * NSA_HEAD_DIM), bf16),
        grid=(n_seq, nc),
        in_specs=[
            pl.BlockSpec((NSA_HEADS, tq, NSA_HEAD_DIM), lambda b, c: (0, b * nc + c, 0)),
            pl.BlockSpec((tq, LANES), lambda b, c: (b * nc + c, 0)),
            kv_spec, kv_spec, kv_spec,
        ],
        out_specs=pl.BlockSpec((tq, NSA_HEADS * NSA_HEAD_DIM), lambda b, c: (b * nc + c, 0)),
        scratch_shapes=[
            pltpu.VMEM((NSA_ROW_W, LANES), bf16),
            pltpu.VMEM((seq_len // kc, NSA_ROW_W, kc), bf16),
            pltpu.VMEM((seq_len // LANES, NSA_ROW_W, LANES), bf16),
        ],
        compiler_params=_cparams("arbitrary", "arbitrary"),
        name="nsa_prompt",
    )(qa, gates, cmp_t, sel_t, win_t)


def _mla_prompt_body(qm_ref, lat_ref, o_ref, lat_sc, *, tq, kc, seq_len):
    c = pl.program_id(1)

    @pl.when(c == 0)
    def _():
        for ch in range(seq_len // kc):
            lat_sc[ch] = lat_ref[0, :, ch * kc:(ch + 1) * kc].astype(bf16)

    rows = MLA_HEADS * tq
    q = qm_ref[...].reshape(rows, MLA_LAT_W)
    t0 = c * tq
    tpos = t0 + lax.broadcasted_iota(jnp.int32, (tq, 1), 0)

    def full_step(ch, carry):
        k_t = lat_sc[ch]
        return _online_step(carry, _dot(q, k_t), k_t[:MLA_KV_RANK])

    n_full = t0 // kc
    carry = lax.fori_loop(0, n_full, full_step, _online_init(rows, MLA_KV_RANK))
    k_t = lat_sc[n_full]
    kpos = n_full * kc + lax.broadcasted_iota(jnp.int32, (1, kc), 1)
    bias = jnp.where(kpos <= tpos, 0.0, NEG_BIG)
    _, l_i, acc = _online_step(carry, _bias_heads(_dot(q, k_t), bias, MLA_HEADS), k_t[:MLA_KV_RANK])
    o = acc * (1.0 / l_i)
    for hh in range(MLA_HEADS):
        o_ref[:, hh * MLA_KV_RANK:(hh + 1) * MLA_KV_RANK] = o[hh * tq:(hh + 1) * tq].astype(o_ref.dtype)


def _mla_prompt(qm, lat_t, *, tq, kc):
    n_seq, _, seq_len = lat_t.shape
    n = n_seq * seq_len
    nc = seq_len // tq
    assert kc % tq == 0
    return pl.pallas_call(
        functools.partial(_mla_prompt_body, tq=tq, kc=kc, seq_len=seq_len),
        out_shape=jax.ShapeDtypeStruct((n, MLA_HEADS * MLA_KV_RANK), bf16),
        grid=(n_seq, nc),
        in_specs=[
            pl.BlockSpec((MLA_HEADS, tq, MLA_LAT_W), lambda b, c: (0, b * nc + c, 0)),
            pl.BlockSpec((1, MLA_LAT_W, seq_len), lambda b, c: (b, 0, 0)),
        ],
        out_specs=pl.BlockSpec((tq, MLA_HEADS * MLA_KV_RANK), lambda b, c: (b * nc + c, 0)),
        scratch_shapes=[pltpu.VMEM((seq_len // kc, MLA_LAT_W, kc), bf16)],
        compiler_params=_cparams("arbitrary", "arbitrary"),
        name="mla_prompt",
    )(qm, lat_t)


PAGES_PER_CHUNK = 16
CHUNK_KEYS = PAGES_PER_CHUNK * PAGE_SIZE


def _page_copy(cache_ref, layer, page, buf, slot, pi, sem):
    return pltpu.make_async_copy(cache_ref.at[page, layer], buf.at[slot, :, pl.ds(pi * LANES, LANES)], sem.at[slot])


def _start_chunk(pt_ref, cache_ref, layer, b, chunk, slot, buf, sem):
    for pi in range(PAGES_PER_CHUNK):
        page = pt_ref[b, chunk * PAGES_PER_CHUNK + pi]
        _page_copy(cache_ref, layer, page, buf, slot, pi, sem).start()


def _wait_chunk(cache_ref, layer, slot, buf, sem):
    for pi in range(PAGES_PER_CHUNK):
        _page_copy(cache_ref, layer, 0, buf, slot, pi, sem).wait()


RING_SLOTS = 4


def _stream_chunks(pt_ref, cache_ref, layer, n_chunks, buf, sem, fn):
    b = pl.program_id(0)
    n_seq = pl.num_programs(0)
    ahead = RING_SLOTS - 1

    def slot_of(seq, chunk):
        return chunk % RING_SLOTS if n_chunks % RING_SLOTS == 0 else (seq * n_chunks + chunk) % RING_SLOTS

    def start(k):
        d_seq, chunk = divmod(k, n_chunks)
        if d_seq == 0:
            _start_chunk(pt_ref, cache_ref, layer, b, chunk, slot_of(b, chunk), buf, sem)
        else:
            @pl.when(b + d_seq < n_seq)
            def _():
                _start_chunk(pt_ref, cache_ref, layer, b + d_seq, chunk, slot_of(b + d_seq, chunk), buf, sem)

    @pl.when(b == 0)
    def _():
        for k in range(ahead):
            start(k)

    for chunk in range(n_chunks):
        start(chunk + ahead)
        slot = slot_of(b, chunk)
        _wait_chunk(cache_ref, layer, slot, buf, sem)
        fn(chunk, buf[slot])


def _cmp_means_body(pt_ref, cache_ref, o_ref, buf, sem, *, layer, n_chunks):
    nbc = CHUNK_KEYS // NSA_BLOCK
    avg = _block_avg_matrix(CHUNK_KEYS, nbc)

    def fn(chunk, slab):
        o_ref[0, :, chunk * nbc:(chunk + 1) * nbc] = _dot(slab.astype(bf16), avg)

    _stream_chunks(pt_ref, cache_ref, layer, n_chunks, buf, sem, fn)


def _cmp_means(page_table, cache_t, layer):
    n_dec, n_pages = page_table.shape
    n_chunks = n_pages // PAGES_PER_CHUNK
    nbp = n_pages * PAGE_SIZE // NSA_BLOCK
    return pl.pallas_call(
        functools.partial(_cmp_means_body, layer=layer, n_chunks=n_chunks),
        out_shape=jax.ShapeDtypeStruct((n_dec, NSA_ROW_W, nbp), f32),
        grid_spec=pltpu.PrefetchScalarGridSpec(
            num_scalar_prefetch=1,
            grid=(n_dec,),
            in_specs=[pl.BlockSpec(memory_space=pl.ANY)],
            out_specs=pl.BlockSpec((1, NSA_ROW_W, nbp), lambda b, pt: (b, 0, 0)),
            scratch_shapes=[pltpu.VMEM((RING_SLOTS, NSA_ROW_W, CHUNK_KEYS), f32),
                            pltpu.SemaphoreType.DMA((RING_SLOTS,))],
        ),
        compiler_params=_cparams("arbitrary"),
        name="cmp_means",
    )(page_table, cache_t)


def _pick_past_blocks(imp, k_past):
    t, nb = imp.shape
    blk = lax.broadcasted_iota(jnp.int32, imp.shape, 1)
    v = jnp.where(blk == 0, jnp.inf, imp)
    v_cols = jnp.concatenate([v, jnp.zeros((LANES - t, nb), f32)], axis=0).T
    lower = lax.broadcasted_iota(jnp.int32, (nb, nb), 0) < lax.broadcasted_iota(jnp.int32, (nb, nb), 1)
    rows = []
    for tt in range(t):
        col = v_cols[:, tt:tt + 1]
        row = v[tt:tt + 1, :]
        beats = (col > row) | ((col == row) & lower)
        rank = jnp.sum(jnp.where(beats, 1.0, 0.0), axis=0, keepdims=True)
        rows.append(jnp.where(rank < k_past, 1.0, 0.0))
    return jnp.concatenate(rows, axis=0)


def _nsa_sample_body(pt_ref, qa_ref, gate_ref, kcm_ref, cmpn_ref, seln_ref, winn_ref, wst_ref, cache_ref,
                     o_ref, buf, sem, s_sc, kv_sc, *, layer, n_chunks, past_len, t_new):
    hd = NSA_HEAD_DIM
    rows = NSA_REP * t_new
    nbp = past_len // NSA_BLOCK
    nbc = CHUNK_KEYS // NSA_BLOCK
    wlen = wst_ref.shape[-1]
    tpos = past_len + lax.broadcasted_iota(jnp.int32, (t_new, 1), 0)
    t_row = lax.broadcasted_iota(jnp.int32, (t_new, 1), 0)
    t_col = lax.broadcasted_iota(jnp.int32, (1, t_new), 1)
    bias_new = jnp.where(t_col <= t_row, 0.0, NEG_BIG)
    gates = gate_ref[...]

    qs, chosen_b, o_cmps = [], [], []
    for g in range(NSA_GROUPS):
        qg = qa_ref[g * NSA_REP:(g + 1) * NSA_REP].reshape(rows, hd).astype(bf16)
        qs.append(qg)
        ks, vs = slice(g * hd, (g + 1) * hd), slice(NSA_KV_W + g * hd, NSA_KV_W + (g + 1) * hd)
        s = _dot(qg, kcm_ref[0, ks, :].astype(bf16))
        cmp_new = jnp.sum(cmpn_ref[0], axis=1, keepdims=True) * (1.0 / NSA_BLOCK)
        s_new = _dot(qg, cmp_new[ks].astype(bf16))
        new_complete = jnp.broadcast_to(((nbp + 1) * NSA_BLOCK - 1 <= tpos)[None],
                                        (NSA_REP, t_new, 1)).reshape(rows, 1)
        m = jnp.maximum(jnp.max(s, axis=-1, keepdims=True), jnp.where(new_complete, s_new, -jnp.inf))
        e = jnp.exp2(s - m)
        e_new = jnp.where(new_complete, jnp.exp2(s_new - m), 0.0)
        den = jnp.maximum(jnp.sum(e, axis=-1, keepdims=True) + e_new, 1e-30)
        p = e / den
        p_new = e_new / den
        o_cmp = _dot_nt(p.astype(bf16), kcm_ref[0, vs, :].astype(bf16))
        o_cmp = o_cmp + _dot_nt(p_new.astype(bf16), cmp_new[vs].astype(bf16))
        o_cmps.append(o_cmp)
        imp = jnp.sum(p.reshape(NSA_REP, t_new, nbp), axis=0)
        chosen = _pick_past_blocks(imp, min(NSA_TOP_K, nbp + 1) - 1)
        chosen_b.append(chosen.astype(bf16))

    expand = _block_expand_matrix(nbc, CHUNK_KEYS, 0)
    zq = jnp.zeros((rows, hd), bf16)
    q_bd = jnp.concatenate([jnp.concatenate([qs[0], zq], axis=1), jnp.concatenate([zq, qs[1]], axis=1)], axis=0)
    chosen_all = jnp.concatenate(chosen_b, axis=0)
    n_hd = NSA_GROUPS * NSA_REP

    def scores(chunk):
        lanes = slice(chunk * CHUNK_KEYS, (chunk + 1) * CHUNK_KEYS)
        hit = _dot(chosen_all[:, chunk * nbc:(chunk + 1) * nbc], expand)
        bias = jnp.where(hit > 0.5, 0.0, NEG_BIG).reshape(NSA_GROUPS, 1, t_new, CHUNK_KEYS)
        s = _dot(q_bd, kv_sc[:NSA_KV_W, lanes]).reshape(NSA_GROUPS, NSA_REP, t_new, CHUNK_KEYS)
        s_sc[:, lanes] = (s + bias).reshape(n_hd * t_new, CHUNK_KEYS)

    def fn(chunk, slab):
        kv_sc[:, chunk * CHUNK_KEYS:(chunk + 1) * CHUNK_KEYS] = slab.astype(bf16)
        if chunk > 0:
            scores(chunk - 1)

    _stream_chunks(pt_ref, cache_ref, layer, n_chunks, buf, sem, fn)
    scores(n_chunks - 1)

    wi = lax.broadcasted_iota(jnp.int32, (1, wlen), 1)
    bias_win = jnp.where(wi > t_row + (wlen - NSA_WINDOW), 0.0, NEG_BIG)
    s_past = s_sc[...]
    s_new = _bias_heads(_dot(q_bd, seln_ref[0, :NSA_KV_W, :].astype(bf16)), bias_new, n_hd)
    m = jnp.maximum(jnp.max(s_past, axis=-1, keepdims=True), jnp.max(s_new, axis=-1, keepdims=True))
    p_past = jnp.exp2(s_past - m)
    p_new = jnp.exp2(s_new - m)
    l_s = jnp.sum(p_past, axis=-1, keepdims=True) + jnp.sum(p_new, axis=-1, keepdims=True)
    acc_s = (_dot_nt(p_past.astype(bf16), kv_sc[NSA_KV_W:, :])
             + _dot_nt(p_new.astype(bf16), seln_ref[0, NSA_KV_W:, :].astype(bf16)))
    o_sel_all = acc_s * (1.0 / l_s)
    for g in range(NSA_GROUPS):
        ks, vs = slice(g * hd, (g + 1) * hd), slice(NSA_KV_W + g * hd, NSA_KV_W + (g + 1) * hd)
        qg = qs[g]
        o_sel = o_sel_all[g * rows:(g + 1) * rows, g * hd:(g + 1) * hd]
        s_w = _bias_heads(_dot(qg, wst_ref[0, 0, ks, :].astype(bf16)), bias_win, NSA_REP)
        s_n = _bias_heads(_dot(qg, winn_ref[0, ks, :].astype(bf16)), bias_new, NSA_REP)
        m = jnp.maximum(jnp.max(s_w, axis=-1, keepdims=True), jnp.max(s_n, axis=-1, keepdims=True))
        p_w = jnp.exp2(s_w - m)
        p_n = jnp.exp2(s_n - m)
        l_w = jnp.sum(p_w, axis=-1, keepdims=True) + jnp.sum(p_n, axis=-1, keepdims=True)
        acc_w = (_dot_nt(p_w.astype(bf16), wst_ref[0, 0, vs, :].astype(bf16))
                 + _dot_nt(p_n.astype(bf16), winn_ref[0, vs, :].astype(bf16)))
        o_win = acc_w * (1.0 / l_w)
        for r in range(NSA_REP):
            hh = g * NSA_REP + r
            sl = slice(r * t_new, (r + 1) * t_new)
            o = (gates[:, hh:hh + 1] * o_cmps[g][sl]
                 + gates[:, NSA_HEADS + hh:NSA_HEADS + hh + 1] * o_sel[sl]
                 + gates[:, 2 * NSA_HEADS + hh:2 * NSA_HEADS + hh + 1] * o_win[sl])
            o_ref[:, hh * hd:(hh + 1) * hd] = o.astype(o_ref.dtype)


def _nsa_sample(page_table, qa, gates, kcm, cmp_new, sel_new, win_new, win_state_t, cache_t, layer, t_new):
    n_dec, n_pages = page_table.shape
    n_chunks = n_pages // PAGES_PER_CHUNK
    past_len = n_pages * PAGE_SIZE
    nbp = past_len // NSA_BLOCK
    wlen = win_state_t.shape[-1]
    new_spec = pl.BlockSpec((1, NSA_ROW_W, t_new), lambda b, pt: (b, 0, 0))
    return pl.pallas_call(
        functools.partial(_nsa_sample_body, layer=layer, n_chunks=n_chunks, past_len=past_len, t_new=t_new),
        out_shape=jax.ShapeDtypeStruct((n_dec * t_new, NSA_HEADS * NSA_HEAD_DIM), f32),
        grid_spec=pltpu.PrefetchScalarGridSpec(
            num_scalar_prefetch=1,
            grid=(n_dec,),
            in_specs=[
                pl.BlockSpec((NSA_HEADS, t_new, NSA_HEAD_DIM), lambda b, pt: (0, b, 0)),
                pl.BlockSpec((t_new, LANES), lambda b, pt: (b, 0)),
                pl.BlockSpec((1, NSA_ROW_W, nbp), lambda b, pt: (b, 0, 0)),
                new_spec, new_spec, new_spec,
                pl.BlockSpec((1, 1, NSA_ROW_W, wlen), lambda b, pt: (layer, b, 0, 0)),
                pl.BlockSpec(memory_space=pl.ANY),
            ],
            out_specs=pl.BlockSpec((t_new, NSA_HEADS * NSA_HEAD_DIM), lambda b, pt: (b, 0)),
            scratch_shapes=[
                pltpu.VMEM((RING_SLOTS, NSA_ROW_W, CHUNK_KEYS), f32),
                pltpu.SemaphoreType.DMA((RING_SLOTS,)),
                pltpu.VMEM((NSA_HEADS * t_new, past_len), f32),
                pltpu.VMEM((NSA_ROW_W, past_len), bf16),
            ],
        ),
        compiler_params=_cparams("arbitrary"),
        name="nsa_sample",
    )(page_table, qa, gates, kcm, cmp_new, sel_new, win_new, win_state_t, cache_t)


def _mla_sample_body(pt_ref, qm_ref, latn_ref, cache_ref, o_ref, buf, sem, s_sc, k_sc, *, layer, n_chunks, t_new):
    rows = MLA_HEADS * t_new
    q = qm_ref[...].reshape(rows, MLA_LAT_W).astype(bf16)

    def scores(chunk):
        lanes = slice(chunk * CHUNK_KEYS, (chunk + 1) * CHUNK_KEYS)
        s_sc[:, lanes] = _dot(q, k_sc[:, lanes])

    def fn(chunk, slab):
        k_sc[:, chunk * CHUNK_KEYS:(chunk + 1) * CHUNK_KEYS] = slab.astype(bf16)
        if chunk > 0:
            scores(chunk - 1)

    _stream_chunks(pt_ref, cache_ref, layer, n_chunks, buf, sem, fn)
    scores(n_chunks - 1)

    t_row = lax.broadcasted_iota(jnp.int32, (t_new, 1), 0)
    t_col = lax.broadcasted_iota(jnp.int32, (1, t_new), 1)
    bias_new = jnp.where(t_col <= t_row, 0.0, NEG_BIG)
    k_new = latn_ref[0].astype(bf16)
    s_past = s_sc[...]
    s_new = _bias_heads(_dot(q, k_new), bias_new, MLA_HEADS)
    m = jnp.maximum(jnp.max(s_past, axis=-1, keepdims=True), jnp.max(s_new, axis=-1, keepdims=True))
    p_past = jnp.exp2(s_past - m)
    p_new = jnp.exp2(s_new - m)
    l_i = jnp.sum(p_past, axis=-1, keepdims=True) + jnp.sum(p_new, axis=-1, keepdims=True)
    acc = (_dot_nt(p_past.astype(bf16), k_sc[:MLA_KV_RANK, :])
           + _dot_nt(p_new.astype(bf16), k_new[:MLA_KV_RANK]))
    o = acc * (1.0 / l_i)
    for hh in range(MLA_HEADS):
        o_ref[:, hh * MLA_KV_RANK:(hh + 1) * MLA_KV_RANK] = o[hh * t_new:(hh + 1) * t_new].astype(o_ref.dtype)


def _mla_sample(page_table, qm, lat_new, cache_t, layer, t_new):
    n_dec, n_pages = page_table.shape
    n_chunks = n_pages // PAGES_PER_CHUNK
    past_len = n_pages * PAGE_SIZE
    return pl.pallas_call(
        functools.partial(_mla_sample_body, layer=layer, n_chunks=n_chunks, t_new=t_new),
        out_shape=jax.ShapeDtypeStruct((n_dec * t_new, MLA_HEADS * MLA_KV_RANK), f32),
        grid_spec=pltpu.PrefetchScalarGridSpec(
            num_scalar_prefetch=1,
            grid=(n_dec,),
            in_specs=[
                pl.BlockSpec((MLA_HEADS, t_new, MLA_LAT_W), lambda b, pt: (0, b, 0)),
                pl.BlockSpec((1, MLA_LAT_W, t_new), lambda b, pt: (b, 0, 0)),
                pl.BlockSpec(memory_space=pl.ANY),
            ],
            out_specs=pl.BlockSpec((t_new, MLA_HEADS * MLA_KV_RANK), lambda b, pt: (b, 0)),
            scratch_shapes=[
                pltpu.VMEM((RING_SLOTS, MLA_LAT_W, CHUNK_KEYS), f32),
                pltpu.SemaphoreType.DMA((RING_SLOTS,)),
                pltpu.VMEM((MLA_HEADS * t_new, past_len), f32),
                pltpu.VMEM((MLA_LAT_W, past_len), bf16),
            ],
        ),
        compiler_params=_cparams("arbitrary"),
        name="mla_sample",
    )(page_table, qm, lat_new, cache_t)


def _rope_tables(pos):
    pos = pos.astype(f32)

    def cs(dim):
        inv = ROPE_THETA ** (-jnp.arange(0, dim, 2, dtype=f32) / dim)
        ang = pos[:, None] * inv[None, :]
        return jnp.cos(ang), jnp.sin(ang)

    ca, sa = cs(NSA_HEAD_DIM)
    cb, sb = cs(MLA_ROPE_DIM)
    rep_a = LANES // NSA_HEAD_DIM
    rep_b = LANES // MLA_ROPE_DIM
    return {
        "ca": jnp.tile(ca, (1, 2 * rep_a)), "sa": jnp.tile(jnp.concatenate([-sa, sa], axis=1), (1, rep_a)),
        "cb": jnp.tile(cb, (1, 2 * rep_b)), "sb": jnp.tile(jnp.concatenate([-sb, sb], axis=1), (1, rep_b)),
        "cat": ca.T, "sat": sa.T, "cbt": cb.T, "sbt": sb.T,
    }


def _layer_weights(l, p):
    d = p["w_in"].shape[1]
    w_in = p["w_in"][l]
    nq = NSA_HEADS * NSA_HEAD_DIM
    sizes = (nq, 6 * NSA_KV_W, N_GATES, MLA_Q_RANK, MLA_KV_RANK, MLA_ROPE_DIM, 2 * d)
    o = np.cumsum((0,) + sizes)
    w_q, w_kv, w_gate, w_cq, w_ckv, w_kr, w_merge = (w_in[:, o[i]:o[i + 1]] for i in range(7))
    w_gate = w_gate.reshape(d, NSA_HEADS, 3).transpose(0, 2, 1).reshape(d, N_GATES)
    w_gate = jnp.pad(w_gate, ((0, 0), (0, LANES - N_GATES)))
    ws = jnp.concatenate([w_q, w_cq, w_gate], axis=1).astype(bf16)
    wt = jnp.concatenate([w_kv, w_ckv, w_kr], axis=1).T.astype(bf16)
    w_uq = p["w_uq"][l].reshape(MLA_Q_RANK, MLA_HEADS, MLA_NOPE_DIM + MLA_ROPE_DIM)
    wuq = jnp.concatenate([w_uq[:, :, :MLA_NOPE_DIM].reshape(MLA_Q_RANK, -1),
                           w_uq[:, :, MLA_NOPE_DIM:].reshape(MLA_Q_RANK, -1)], axis=1).astype(bf16)
    eye = jnp.eye(MLA_HEADS, dtype=f32)
    wuk_bd = jnp.einsum("chn,hk->hnkc", p["w_uk"][l], eye).reshape(
        MLA_HEADS * MLA_NOPE_DIM, MLA_HEADS * MLA_KV_RANK).astype(bf16)
    wuv_bd = jnp.einsum("chv,hk->hckv", p["w_uv"][l], eye).reshape(
        MLA_HEADS * MLA_KV_RANK, MLA_HEADS * MLA_V_DIM).astype(bf16)
    return {
        "g_ffn_a": p["g_ffn_a"][l][None], "w13_a": p["w13_a"][l].astype(bf16), "w2_a": p["w2_a"][l].astype(bf16),
        "g_ffn_b": p["g_ffn_b"][l][None], "w13_b": p["w13_b"][l].astype(bf16), "w2_b": p["w2_b"][l].astype(bf16),
        "g_mix": p["g_mix"][l][None], "ws": ws, "wt": wt, "wm": w_merge.astype(bf16),
        "g_cq": p["g_cq"][l][None], "wuq": wuq, "wuk_bd": wuk_bd, "g_ckv": p["g_ckv"][l][:, None],
        "wuv_bd": wuv_bd, "w_o_nsa": p["w_o_nsa"][l].astype(bf16), "w_o_mla": p["w_o_mla"][l].astype(bf16),
        "w_out": p["w_out"][l].astype(bf16),
    }


def _tile(n, want):
    t = min(n, want)
    assert n % t == 0, (n, want)
    return t


FFN_SPLIT = 2


def _ffn_pair(x, wl, which, gf, final_norm):
    n = x.shape[0]
    dff = wl["w2_" + which].shape[0]
    n_split = FFN_SPLIT if dff % (FFN_SPLIT * LANES) == 0 else 1
    return _ffn(x, wl["g_ffn_" + which], wl["w13_" + which], wl["w2_" + which], gf,
                final_norm=final_norm, tm=_tile(n, 512), n_split=n_split)


def _prompt_trunk(x_prompt, wls, g_final):
    bsz, seq_len, d = x_prompt.shape
    assert seq_len % LANES == 0 and seq_len % NSA_BLOCK == 0
    depth = len(wls)
    n = bsz * seq_len
    x = x_prompt.reshape(n, d)
    tabs = _rope_tables(jnp.arange(seq_len, dtype=jnp.int32))
    tm = _tile(seq_len, 512)
    tq = LANES
    kc = _tile(seq_len, 512)
    cmp_rows, sel_rows, lat_rows, win_rows = [], [], [], []
    for l, wl in enumerate(wls):
        x = _ffn_pair(x, wl, "a", g_final, False)
        qa, gates, qm, cmp_t, sel_t, win_t, lat_t = _proj(x, seq_len, wl, tabs, tm=tm, q_dtype=bf16)
        o_nsa = _nsa_prompt(qa, gates, cmp_t, sel_t, win_t, tq=tq, kc=kc)
        o_lat = _mla_prompt(qm, lat_t, tq=tq, kc=kc)
        x = _merge(x, o_nsa, o_lat, wl, tm=tm)
        x = _ffn_pair(x, wl, "b", g_final, l == depth - 1)
        cmp_rows.append(cmp_t)
        sel_rows.append(sel_t)
        lat_rows.append(lat_t)
        win_rows.append(win_t)
    return x.reshape(bsz, seq_len, d), cmp_rows, sel_rows, lat_rows, win_rows


def _per_seq(rows_t, n_dec, t_new):
    f = rows_t.shape[1]
    return jnp.transpose(rows_t[0].reshape(f, n_dec, t_new), (1, 0, 2))


def _sample_trunk(x_sample, cmp_c, sel_c, mla_c, win_state_t, page_table, wls, g_final):
    n_dec, t_new, d = x_sample.shape
    n_pages = page_table.shape[1]
    assert t_new < NSA_BLOCK and n_pages % PAGES_PER_CHUNK == 0 and PAGE_SIZE % NSA_BLOCK == 0
    depth = len(wls)
    n = n_dec * t_new
    past_len = n_pages * PAGE_SIZE
    x = x_sample.reshape(n, d)
    pos = past_len + jnp.arange(t_new, dtype=jnp.int32)
    tabs = _rope_tables(jnp.tile(pos, n_dec))
    tm = _tile(n, 512)
    cmp_rows, sel_rows, lat_rows, win_rows = [], [], [], []
    for l, wl in enumerate(wls):
        x = _ffn_pair(x, wl, "a", g_final, False)
        qa, gates, qm, cmp_t, sel_t, win_t, lat_t = _proj(x, n, wl, tabs, tm=tm, q_dtype=f32)
        cmp_n, sel_n, win_n, lat_n = (_per_seq(a, n_dec, t_new) for a in (cmp_t, sel_t, win_t, lat_t))
        kcm = _cmp_means(page_table, cmp_c, l)
        o_nsa = _nsa_sample(page_table, qa, gates, kcm, cmp_n, sel_n, win_n, win_state_t, sel_c, l, t_new)
        o_lat = _mla_sample(page_table, qm, lat_n, mla_c, l, t_new)
        x = _merge(x, o_nsa, o_lat, wl, tm=tm)
        x = _ffn_pair(x, wl, "b", g_final, l == depth - 1)
        cmp_rows.append(cmp_n)
        sel_rows.append(sel_n)
        lat_rows.append(lat_n)
        win_rows.append(jnp.concatenate([win_state_t[l][:, :, t_new:], win_n], axis=-1))
    return x.reshape(n_dec, t_new, d), cmp_rows, sel_rows, lat_rows, win_rows


def _kv_rows_out(rows_t):
    t = jnp.stack(rows_t, axis=1)
    b, depth, _, s = t.shape
    t = t.reshape(b, depth, 2, NSA_GROUPS, NSA_HEAD_DIM, s)
    return jnp.transpose(t, (0, 5, 1, 2, 3, 4))


def _lat_rows_out(rows_t):
    return jnp.transpose(jnp.stack(rows_t, axis=1), (0, 3, 1, 2))


def _win_out(rows_t):
    t = jnp.stack(rows_t, axis=0)
    depth, b, _, w = t.shape
    t = t.reshape(depth, b, 2, NSA_GROUPS, NSA_HEAD_DIM, w)
    return jnp.transpose(t, (0, 1, 5, 2, 3, 4))


def _last_rows(rows_t, wlen):
    s = rows_t.shape[-1]
    if s >= wlen:
        return rows_t[:, :, s - wlen:]
    return jnp.pad(rows_t, ((0, 0), (0, 0), (wlen - s, 0)))


def kernel(x_prompt, x_sample, cache_nsa_cmp, cache_nsa_sel, cache_mla, state_nsa_win, page_table,
           g_ffn_a, w13_a, w2_a, g_mix, w_in, g_cq, w_uq, g_ckv, w_uk, w_uv,
           w_o_nsa, w_o_mla, w_out, g_ffn_b, w13_b, w2_b, g_final):
    p = dict(g_ffn_a=g_ffn_a, w13_a=w13_a, w2_a=w2_a, g_mix=g_mix, w_in=w_in, g_cq=g_cq, w_uq=w_uq,
             g_ckv=g_ckv, w_uk=w_uk, w_uv=w_uv, w_o_nsa=w_o_nsa, w_o_mla=w_o_mla, w_out=w_out,
             g_ffn_b=g_ffn_b, w13_b=w13_b, w2_b=w2_b)
    assert cache_nsa_cmp.shape[1] == PAGE_SIZE
    n_pool, _, depth = cache_nsa_cmp.shape[:3]
    wlen = state_nsa_win.shape[2]
    wls = [_layer_weights(l, p) for l in range(depth)]
    gf = g_final[None]
    cmp_c = jnp.transpose(cache_nsa_cmp, (0, 2, 3, 4, 5, 1)).reshape(n_pool, depth, NSA_ROW_W, PAGE_SIZE)
    sel_c = jnp.transpose(cache_nsa_sel, (0, 2, 3, 4, 5, 1)).reshape(n_pool, depth, NSA_ROW_W, PAGE_SIZE)
    mla_c = jnp.transpose(cache_mla, (0, 2, 3, 1))
    win_state_t = jnp.transpose(state_nsa_win, (0, 1, 3, 4, 5, 2)).reshape(
        depth, state_nsa_win.shape[1], NSA_ROW_W, wlen)

    y_p, cmp_p, sel_p, lat_p, win_p = _prompt_trunk(x_prompt, wls, gf)
    y_s, cmp_s, sel_s, lat_s, win_s = _sample_trunk(x_sample, cmp_c, sel_c, mla_c, win_state_t, page_table, wls, gf)
    win_p = [_last_rows(w, wlen) for w in win_p]
    return (y_p, y_s,
            _kv_rows_out(cmp_p), _kv_rows_out(cmp_s),
            _kv_rows_out(sel_p), _kv_rows_out(sel_s),
            _lat_rows_out(lat_p), _lat_rows_out(lat_s),
            _win_out(win_p), _win_out(win_s))
```
